```python
import math
import jax, jax.numpy as jnp
from jax import lax
import numpy as np

D_MODEL = 1024
BATCH = 16
SEQ = 4096
DEPTH = 4

PLE_DIM = 256
A_GROUPS = 8
A_GROUP_DIM = 128
A_WIDTH = A_GROUPS * A_GROUP_DIM
CHUNK = 128
B_HEADS = 8
B_HEAD_DIM = 64
B_V_DIM = 2 * B_HEAD_DIM
B_QK_WIDTH = B_HEADS * 2 * B_HEAD_DIM
B_WIDTH = B_HEADS * B_V_DIM
Q_BLOCK = 128
N_IN = 2 * A_WIDTH + 2 * B_QK_WIDTH + B_WIDTH + 2 * D_MODEL
D_FF = 2816
CONV_W = 3
EPS = 1e-6

kernel_name = "hybrid_gmlp_diffattn_convffn_ple"


def rmsnorm(x, g):
    xf = x.astype(jnp.float32)
    y = xf * lax.rsqrt(jnp.mean(xf * xf, axis=-1, keepdims=True) + EPS)
    return (y * g.astype(jnp.float32)).astype(x.dtype)


def layernorm(x, g, b):
    xf = x.astype(jnp.float32)
    mu = jnp.mean(xf, axis=-1, keepdims=True)
    var = jnp.mean(jnp.square(xf - mu), axis=-1, keepdims=True)
    y = (xf - mu) * lax.rsqrt(var + EPS)
    return (y * g.astype(jnp.float32) + b.astype(jnp.float32)).astype(x.dtype)


def gelu(x):
    return jax.nn.gelu(x, approximate=True)


def spatial_gating_mixer(z, ln_g, ln_b, w_s, b_s):
    z = gelu(z)
    u, v = jnp.split(z, 2, axis=-1)
    v = layernorm(v, ln_g, ln_b)
    bsz, s_len, _ = v.shape
    n_chunks = s_len // CHUNK
    v = v.reshape(bsz, n_chunks, CHUNK, A_GROUPS, A_GROUP_DIM)
    causal = jnp.tril(jnp.ones((CHUNK, CHUNK), dtype=bool))
    w = jnp.where(causal[None], w_s, jnp.zeros((), w_s.dtype))
    s = jnp.einsum('gts,bnsgc->bntgc', w, v) + b_s.T[:, :, None]
    return u * s.reshape(bsz, s_len, A_WIDTH)


def diff_attention(q, k, v, lam, g_sub, lam_init):
    bsz, s_len, _ = q.shape
    q = q.reshape(bsz, s_len, B_HEADS, 2, B_HEAD_DIM) * (B_HEAD_DIM ** -0.5)
    k = k.reshape(bsz, s_len, B_HEADS, 2, B_HEAD_DIM)
    v = v.reshape(bsz, s_len, B_HEADS, B_V_DIM)
    n_blocks = s_len // Q_BLOCK
    q_blocks = jnp.moveaxis(q.reshape(bsz, n_blocks, Q_BLOCK, B_HEADS, 2, B_HEAD_DIM), 1, 0)
    starts = jnp.arange(n_blocks, dtype=jnp.int32) * Q_BLOCK
    kpos = jnp.arange(s_len, dtype=jnp.int32)

    def block(args):
        q_blk, start = args
        sc = jnp.einsum('bqhmd,bkhmd->bmhqk', q_blk, k).astype(jnp.float32)
        qpos = start + jnp.arange(Q_BLOCK, dtype=jnp.int32)
        causal = kpos[None, :] <= qpos[:, None]
        sc = jnp.where(causal, sc, -jnp.inf)
        pr = jax.nn.softmax(sc, axis=-1)
        a = pr[:, 0] - lam * pr[:, 1]
        return jnp.einsum('bhqk,bkhe->bqhe', a.astype(v.dtype), v)

    o = lax.map(block, (q_blocks, starts))
    o = jnp.moveaxis(o, 0, 1).reshape(bsz, s_len, B_HEADS, B_V_DIM)
    o = rmsnorm(o, g_sub) * (1.0 - lam_init)
    return o.reshape(bsz, s_len, B_WIDTH)


def causal_dwconv(a, w, b):
    s_len = a.shape[1]
    ap = jnp.pad(a, ((0, 0), (CONV_W - 1, 0), (0, 0)))
    y = b
    for j in range(CONV_W):
        y = y + ap[:, j:j + s_len] * w[j]
    return y


def setup_inputs(seed: int = 0) -> dict:
    key = jax.random.key(seed)
    ks = jax.random.split(key, 32)
    f32 = jnp.float32
    nrm = lambda k, shape, scale: jax.random.normal(k, shape, f32) * scale
    gain = lambda k, shape: 1.0 + 0.02 * jax.random.normal(k, shape, f32)
    L = DEPTH
    return {
        "x": nrm(ks[0], (BATCH, SEQ, D_MODEL), 1.0),
        "p": nrm(ks[1], (DEPTH, BATCH, SEQ, PLE_DIM), 1.0),
        "g_mix": gain(ks[2], (L, D_MODEL)),
        "w_in": nrm(ks[3], (L, D_MODEL, N_IN), D_MODEL ** -0.5),
        "b_gate": nrm(ks[4], (L, 2 * D_MODEL), 0.02),
        "ln_v_g": gain(ks[5], (L, A_WIDTH)),
        "ln_v_b": nrm(ks[6], (L, A_WIDTH), 0.02),
        "w_s": nrm(ks[7], (L, A_GROUPS, CHUNK, CHUNK), CHUNK ** -0.5),
        "b_s": gain(ks[8], (L, A_GROUPS, CHUNK)),
        "lam_q1": nrm(ks[9], (L, B_HEAD_DIM), 0.1),
        "lam_k1": nrm(ks[10], (L, B_HEAD_DIM), 0.1),
        "lam_q2": nrm(ks[11], (L, B_HEAD_DIM), 0.1),
        "lam_k2": nrm(ks[12], (L, B_HEAD_DIM), 0.1),
        "g_sub": gain(ks[13], (L, B_V_DIM)),
        "w_pa": nrm(ks[14], (L, A_WIDTH, D_MODEL), A_WIDTH ** -0.5),
        "w_pb": nrm(ks[15], (L, B_WIDTH, D_MODEL), B_WIDTH ** -0.5),
        "w_o": nrm(ks[16], (L, D_MODEL, D_MODEL), D_MODEL ** -0.5),
        "g_ffn": gain(ks[17], (L, D_MODEL)),
        "w_up": nrm(ks[18], (L, D_MODEL, 2 * D_FF), D_MODEL ** -0.5),
        "conv_w": nrm(ks[19], (L, CONV_W, 2 * D_FF), CONV_W ** -0.5),
        "conv_b": nrm(ks[20], (L, 2 * D_FF), 0.02),
        "w_down": nrm(ks[21], (L, D_FF, D_MODEL), D_FF ** -0.5),
        "g_ple": gain(ks[22], (L, D_MODEL)),
        "w_pg": nrm(ks[23], (L, D_MODEL, D_MODEL), D_MODEL ** -0.5),
        "b_pg": nrm(ks[24], (L, D_MODEL), 0.02),
        "w_ple": nrm(ks[25], (L, PLE_DIM, D_MODEL), PLE_DIM ** -0.5),
        "g_final": gain(ks[26], (D_MODEL,)),
    }


def reference(x, p, g_mix, w_in, b_gate, ln_v_g, ln_v_b, w_s, b_s, lam_q1, lam_k1, lam_q2, lam_k2,
              g_sub, w_pa, w_pb, w_o, g_ffn, w_up, conv_w, conv_b, w_down, g_ple, w_pg, b_pg,
              w_ple, g_final):
    o_q = 2 * A_WIDTH
    o_k = o_q + B_QK_WIDTH
    o_v = o_k + B_QK_WIDTH
    o_g = o_v + B_WIDTH
    for i in range(DEPTH):
        lam_init = 0.8 - 0.6 * math.exp(-0.3 * i)
        h = rmsnorm(x, g_mix[i])
        z = h @ w_in[i]
        gates = jax.nn.sigmoid(z[..., o_g:] + b_gate[i])
        gate_a, gate_b = jnp.split(gates, 2, axis=-1)
        y_a = spatial_gating_mixer(z[..., :o_q], ln_v_g[i], ln_v_b[i], w_s[i], b_s[i])
        lam = (jnp.exp(jnp.sum(lam_q1[i].astype(jnp.float32) * lam_k1[i].astype(jnp.float32)))
               - jnp.exp(jnp.sum(lam_q2[i].astype(jnp.float32) * lam_k2[i].astype(jnp.float32)))
               + lam_init)
        y_b = diff_attention(z[..., o_q:o_k], z[..., o_k:o_v], z[..., o_v:o_g], lam, g_sub[i], lam_init)
        merged = gate_a * (y_a @ w_pa[i]) + gate_b * (y_b @ w_pb[i])
        x = x + merged @ w_o[i]
        h2 = rmsnorm(x, g_ffn[i])
        a = causal_dwconv(h2 @ w_up[i], conv_w[i], conv_b[i])
        a_g, a_v = jnp.split(a, 2, axis=-1)
        x = x + (gelu(a_g) * a_v) @ w_down[i]
        e = p[i] @ w_ple[i]
        pg = jax.nn.sigmoid(rmsnorm(x, g_ple[i]) @ w_pg[i] + b_pg[i])
        x = x + pg * e
    return rmsnorm(x, g_final)
```

```python
import functools
import math

import jax
import jax.numpy as jnp
from jax import lax
from jax.experimental import pallas as pl
from jax.experimental.pallas import tpu as pltpu

D_MODEL = 1024
PLE_DIM = 256
A_GROUPS = 8
A_GROUP_DIM = 128
A_WIDTH = A_GROUPS * A_GROUP_DIM
CHUNK = 128
B_HEADS = 8
B_HEAD_DIM = 64
B_V_DIM = 2 * B_HEAD_DIM
B_QK_WIDTH = B_HEADS * 2 * B_HEAD_DIM
B_WIDTH = B_HEADS * B_V_DIM
N_IN = 2 * A_WIDTH + 2 * B_QK_WIDTH + B_WIDTH + 2 * D_MODEL
D_FF = 2816
CONV_W = 3
EPS = 1e-6

O_Q = 2 * A_WIDTH
O_K = O_Q + B_QK_WIDTH
O_V = O_K + B_QK_WIDTH
O_G = O_V + B_WIDTH

_MXU_DTYPE = jnp.bfloat16

_V7X_VMEM_BYTES = 64 * 1024 * 1024
_VMEM_LIMIT = _V7X_VMEM_BYTES - 8 * 1024 * 1024

_TM_PRE = 256
_TM_POST = 256
_TQ = 256
_FF_CHUNK = 256
_HALO = 16


def _rms(x, g):
    return x * lax.rsqrt(jnp.mean(x * x, axis=-1, keepdims=True) + EPS) * g


def _gelu(x):
    return jax.nn.gelu(x, approximate=True)


def _dot(a, b):
    return jnp.dot(a, b, preferred_element_type=jnp.float32)


def _resident(shape, layer):
    nd = len(shape)
    return pl.BlockSpec((None,) + tuple(shape), lambda *_: (layer,) + (0,) * nd,
                        pipeline_mode=pl.Buffered(1))


def _pre_kernel(x_ref, gmix_ref, win_ref, bgate_ref, lng_ref, lnb_ref, ws_ref, bst_ref, wpa_ref,
                q_ref, k_ref, v_ref, ma_ref, gb_ref, u_sc, vn_sc, ya_sc):
    tm = x_ref.shape[0]
    h = _rms(x_ref[...], gmix_ref[...]).astype(_MXU_DTYPE)

    def proj(lo, width):
        return _dot(h, win_ref[:, lo:lo + width])

    u_sc[...] = _gelu(proj(0, A_WIDTH))
    v = _gelu(proj(A_WIDTH, A_WIDTH))
    mu = jnp.mean(v, axis=-1, keepdims=True)
    vc = v - mu
    var = jnp.mean(vc * vc, axis=-1, keepdims=True)
    vn_sc[...] = (vc * lax.rsqrt(var + EPS) * lng_ref[...] + lnb_ref[...]).astype(_MXU_DTYPE)
    causal = (lax.broadcasted_iota(jnp.int32, (CHUNK, CHUNK), 1)
              <= lax.broadcasted_iota(jnp.int32, (CHUNK, CHUNK), 0))
    for g in range(A_GROUPS):
        cs = slice(g * A_GROUP_DIM, (g + 1) * A_GROUP_DIM)
        w = jnp.where(causal, ws_ref[g], 0.0).astype(_MXU_DTYPE)
        b = bst_ref[:, g:g + 1]
        for c in range(tm // CHUNK):
            rs = slice(c * CHUNK, (c + 1) * CHUNK)
            s = _dot(w, vn_sc[rs, cs]) + b
            ya_sc[rs, cs] = (u_sc[rs, cs] * s).astype(_MXU_DTYPE)

    gate_a = jax.nn.sigmoid(proj(O_G, D_MODEL) + bgate_ref[:, :D_MODEL])
    ma_ref[...] = (gate_a * _dot(ya_sc[...], wpa_ref[...])).astype(ma_ref.dtype)
    gb_ref[...] = jax.nn.sigmoid(proj(O_G + D_MODEL, D_MODEL) + bgate_ref[:, D_MODEL:]).astype(gb_ref.dtype)
    q_ref[...] = (proj(O_Q, B_QK_WIDTH) * (B_HEAD_DIM ** -0.5)).astype(q_ref.dtype)
    k_ref[...] = proj(O_K, B_QK_WIDTH).astype(k_ref.dtype)
    v_ref[...] = proj(O_V, B_WIDTH).astype(v_ref.dtype)


def _pre_call(x, layer, g_mix, w_in, b_gate, ln_v_g, ln_v_b, w_s, b_s_t, w_pa):
    t = x.shape[0]
    tm = min(_TM_PRE, t)
    tok = lambda w: pl.BlockSpec((tm, w), lambda i: (i, 0))
    act = jax.ShapeDtypeStruct((t, D_MODEL), _MXU_DTYPE)
    return pl.pallas_call(
        _pre_kernel,
        grid=(t // tm,),
        in_specs=[tok(D_MODEL),
                  _resident((1, D_MODEL), layer),
                  _resident((D_MODEL, N_IN), layer),
                  _resident((1, 2 * D_MODEL), layer),
                  _resident((1, A_WIDTH), layer),
                  _resident((1, A_WIDTH), layer),
                  _resident((A_GROUPS, CHUNK, CHUNK), layer),
                  _resident((CHUNK, A_GROUPS), layer),
                  _resident((A_WIDTH, D_MODEL), layer)],
        out_specs=[tok(D_MODEL)] * 5,
        out_shape=[act] * 5,
        scratch_shapes=[pltpu.VMEM((tm, A_WIDTH), jnp.float32),
                        pltpu.VMEM((tm, A_WIDTH), _MXU_DTYPE),
                        pltpu.VMEM((tm, A_WIDTH), _MXU_DTYPE)],
        compiler_params=pltpu.CompilerParams(dimension_semantics=("parallel",),
                                             vmem_limit_bytes=_VMEM_LIMIT),
        name=f"pre_l{layer}",
    )(x, g_mix, w_in, b_gate, ln_v_g, ln_v_b, w_s, b_s_t, w_pa)


def _attn_kernel(q_ref, k_ref, v_ref, lq1_ref, lk1_ref, lq2_ref, lk2_ref, gsub_ref, o_ref, *, lam_init):
    tq = q_ref.shape[0]
    qi = pl.program_id(2)
    q = q_ref[...]
    lane = lax.broadcasted_iota(jnp.int32, q.shape, 1)
    q1 = jnp.where(lane < B_HEAD_DIM, q, jnp.zeros_like(q))
    q2 = jnp.where(lane >= B_HEAD_DIM, q, jnp.zeros_like(q))
    lam = (jnp.exp(jnp.sum(lq1_ref[...] * lk1_ref[...], axis=-1, keepdims=True))
           - jnp.exp(jnp.sum(lq2_ref[...] * lk2_ref[...], axis=-1, keepdims=True)) + lam_init)

    nt = (((1,), (1,)), ((), ()))

    def update(s, vt, m, l, acc):
        m_new = jnp.maximum(m, jnp.max(s, axis=-1, keepdims=True))
        alpha = jnp.exp(m - m_new)
        p = jnp.exp(s - m_new)
        l_new = alpha * l + jnp.sum(p, axis=-1, keepdims=True)
        acc_new = alpha * acc + _dot(p.astype(vt.dtype), vt)
        return m_new, l_new, acc_new

    def tile(j, carry, masked):
        m1, l1, a1, m2, l2, a2 = carry
        ks = pl.multiple_of(j * tq, tq)
        kt = k_ref[pl.ds(ks, tq), :]
        vt = v_ref[pl.ds(ks, tq), :]
        s1 = lax.dot_general(q1, kt, nt, preferred_element_type=jnp.float32)
        s2 = lax.dot_general(q2, kt, nt, preferred_element_type=jnp.float32)
        if masked:
            keep = (lax.broadcasted_iota(jnp.int32, s1.shape, 1)
                    <= lax.broadcasted_iota(jnp.int32, s1.shape, 0))
            s1 = jnp.where(keep, s1, -jnp.inf)
            s2 = jnp.where(keep, s2, -jnp.inf)
        m1, l1, a1 = update(s1, vt, m1, l1, a1)
        m2, l2, a2 = update(s2, vt, m2, l2, a2)
        return m1, l1, a1, m2, l2, a2

    m0 = jnp.full((tq, 1), -jnp.inf, jnp.float32)
    l0 = jnp.zeros((tq, 1), jnp.float32)
    a0 = jnp.zeros((tq, B_V_DIM), jnp.float32)
    carry = tile(qi, (m0, l0, a0, m0, l0, a0), True)
    m1, l1, a1, m2, l2, a2 = lax.fori_loop(0, qi, lambda j, c: tile(j, c, False), carry)
    o = a1 / l1 - lam * (a2 / l2)
    o_ref[...] = (_rms(o, gsub_ref[...]) * (1.0 - lam_init)).astype(o_ref.dtype)


def _attn_call(q, k, v, layer, lam_init, lam_q1, lam_k1, lam_q2, lam_k2, g_sub):
    bsz, s_len, _ = q.shape
    tq = min(_TQ, s_len)
    qspec = pl.BlockSpec((None, tq, B_V_DIM), lambda b, h, i: (b, i, h))
    kvspec = pl.BlockSpec((None, s_len, B_V_DIM), lambda b, h, i: (b, 0, h))
    vec = lambda n: pl.BlockSpec((None, 1, n), lambda b, h, i: (layer, 0, 0))
    return pl.pallas_call(
        functools.partial(_attn_kernel, lam_init=lam_init),
        grid=(bsz, B_HEADS, s_len // tq),
        in_specs=[qspec, kvspec, kvspec, vec(B_HEAD_DIM), vec(B_HEAD_DIM), vec(B_HEAD_DIM), vec(B_HEAD_DIM),
                  vec(B_V_DIM)],
        out_specs=qspec,
        out_shape=jax.ShapeDtypeStruct((bsz, s_len, B_WIDTH), _MXU_DTYPE),
        compiler_params=pltpu.CompilerParams(dimension_semantics=("parallel", "parallel", "parallel"),
                                             vmem_limit_bytes=_VMEM_LIMIT),
        name=f"attn_l{layer}",
    )(q, k, v, lam_q1, lam_k1, lam_q2, lam_k2, g_sub)


def _post_kernel(x_ref, ma_ref, gb_ref, yb_ref, p_ref, wpb_ref, wo_ref, gffn_ref, wup_ref, cw_ref, cb_ref,
                 wdown_ref, gple_ref, wpg_ref, bpg_ref, wple_ref, gfin_ref, o_ref, halo_sc, act_sc,
                 *, tiles_per_seq, final):
    tm = x_ref.shape[0]

    @pl.when(pl.program_id(0) % tiles_per_seq == 0)
    def _():
        halo_sc[...] = jnp.zeros_like(halo_sc)

    pb = _dot(yb_ref[...], wpb_ref[...])
    merged = ma_ref[...].astype(jnp.float32) + gb_ref[...].astype(jnp.float32) * pb
    x1 = x_ref[...] + _dot(merged.astype(_MXU_DTYPE), wo_ref[...])

    h2 = _rms(x1, gffn_ref[...]).astype(_MXU_DTYPE)
    top = lax.broadcasted_iota(jnp.int32, (_HALO, _FF_CHUNK), 0)

    def conv(lo):
        a = _dot(h2, wup_ref[:, lo:lo + _FF_CHUNK])
        w0, w1, w2 = (cw_ref[j:j + 1, lo:lo + _FF_CHUNK] for j in range(CONV_W))
        b = cb_ref[:, lo:lo + _FF_CHUNK]
        y = b + pltpu.roll(a, 2, axis=0) * w0 + pltpu.roll(a, 1, axis=0) * w1 + a * w2
        prev, head = halo_sc[:, lo:lo + _FF_CHUNK], a[:_HALO]
        r2 = jnp.where(top < 2, pltpu.roll(prev, 2, axis=0), pltpu.roll(head, 2, axis=0))
        r1 = jnp.where(top < 1, pltpu.roll(prev, 1, axis=0), pltpu.roll(head, 1, axis=0))
        y_head = b + r2 * w0 + r1 * w1 + head * w2
        halo_sc[:, lo:lo + _FF_CHUNK] = a[tm - _HALO:]
        return y, y_head

    for j in range(D_FF // _FF_CHUNK):
        lo = j * _FF_CHUNK
        yg, yg_head = conv(lo)
        yv, yv_head = conv(D_FF + lo)
        act_sc[:, lo:lo + _FF_CHUNK] = (_gelu(yg) * yv).astype(_MXU_DTYPE)
        act_sc[:_HALO, lo:lo + _FF_CHUNK] = (_gelu(yg_head) * yv_head).astype(_MXU_DTYPE)
    x2 = x1 + _dot(act_sc[...], wdown_ref[...])

    e = _dot(p_ref[...].astype(_MXU_DTYPE), wple_ref[...])
    pg = jax.nn.sigmoid(_dot(_rms(x2, gple_ref[...]).astype(_MXU_DTYPE), wpg_ref[...]) + bpg_ref[...])
    x3 = x2 + pg * e
    if final:
        x3 = _rms(x3, gfin_ref[...])
    o_ref[...] = x3


def _post_call(x, ma, gb, yb, p, layer, s_len, final, w_pb, w_o, g_ffn, w_up, conv_w, conv_b, w_down, g_ple,
               w_pg, b_pg, w_ple, g_final):
    t = x.shape[0]
    tm = min(_TM_POST, s_len)
    tok = lambda w: pl.BlockSpec((tm, w), lambda i: (i, 0))
    return pl.pallas_call(
        functools.partial(_post_kernel, tiles_per_seq=s_len // tm, final=final),
        grid=(t // tm,),
        in_specs=[tok(D_MODEL), tok(D_MODEL), tok(D_MODEL), tok(B_WIDTH),
                  pl.BlockSpec((None, tm, PLE_DIM), lambda i: (layer, i, 0)),
                  _resident((B_WIDTH, D_MODEL), layer),
                  _resident((D_MODEL, D_MODEL), layer),
                  _resident((1, D_MODEL), layer),
                  _resident((D_MODEL, 2 * D_FF), layer),
                  _resident((CONV_W, 2 * D_FF), layer),
                  _resident((1, 2 * D_FF), layer),
                  _resident((D_FF, D_MODEL), layer),
                  _resident((1, D_MODEL), layer),
                  _resident((D_MODEL, D_MODEL), layer),
                  _resident((1, D_MODEL), layer),
                  _resident((PLE_DIM, D_MODEL), layer),
                  pl.BlockSpec((1, D_MODEL), lambda i: (0, 0))],
        out_specs=tok(D_MODEL),
        out_shape=jax.ShapeDtypeStruct((t, D_MODEL), jnp.float32),
        scratch_shapes=[pltpu.VMEM((_HALO, 2 * D_FF), jnp.float32),
                        pltpu.VMEM((tm, D_FF), _MXU_DTYPE)],
        compiler_params=pltpu.CompilerParams(dimension_semantics=("arbitrary",),
                                             vmem_limit_bytes=_VMEM_LIMIT),
        name=f"post_l{layer}",
    )(x, ma, gb, yb, p, w_pb, w_o, g_ffn, w_up, conv_w, conv_b, w_down, g_ple, w_pg, b_pg, w_ple, g_final)


def kernel(x, p, g_mix, w_in, b_gate, ln_v_g, ln_v_b, w_s, b_s, lam_q1, lam_k1, lam_q2, lam_k2, g_sub, w_pa,
           w_pb, w_o, g_ffn, w_up, conv_w, conv_b, w_down, g_ple, w_pg, b_pg, w_ple, g_final):
    bsz, s_len, _ = x.shape
    depth = w_in.shape[0]
    t = bsz * s_len
    mx = lambda w: w.astype(_MXU_DTYPE)
    row = lambda a: a[:, None, :]
    w_in, w_pa, w_pb, w_o, w_up, w_down, w_pg, w_ple = map(mx, (w_in, w_pa, w_pb, w_o, w_up, w_down, w_pg, w_ple))
    g_mix, b_gate, ln_v_g, ln_v_b, g_ffn, conv_b, g_ple, b_pg = map(
        row, (g_mix, b_gate, ln_v_g, ln_v_b, g_ffn, conv_b, g_ple, b_pg))
    lam_q1, lam_k1, lam_q2, lam_k2, g_sub = map(row, (lam_q1, lam_k1, lam_q2, lam_k2, g_sub))
    b_s_t = jnp.swapaxes(b_s, 1, 2)
    g_final = g_final[None, :]
    p = p.reshape(depth, t, PLE_DIM)
    x = x.reshape(t, D_MODEL)
    for i in range(depth):
        lam_init = 0.8 - 0.6 * math.exp(-0.3 * i)
        q, k, v, ma, gb = _pre_call(x, i, g_mix, w_in, b_gate, ln_v_g, ln_v_b, w_s, b_s_t, w_pa)
        shp = (bsz, s_len, D_MODEL)
        yb = _attn_call(q.reshape(shp), k.reshape(shp), v.reshape(shp), i, lam_init, lam_q1, lam_k1, lam_q2,
                        lam_k2, g_sub)
        x = _post_call(x, ma, gb, yb.reshape(t, B_WIDTH), p, i, s_len, i == depth - 1, w_pb, w_o, g_ffn, w_up,
                       conv_w, conv_b, w_down, g_ple, w_pg, b_pg, w_ple, g_final)
    return x.reshape(bsz, s_len, D_MODEL)
```

```python
import functools
import math

import jax
import jax.numpy as jnp
from jax import lax
from jax.experimental import pallas as pl
from jax.experimental.pallas import tpu as pltpu

D_MODEL = 1024
PLE_DIM = 256
A_GROUPS = 8
A_GROUP_DIM = 128
A_WIDTH = A_GROUPS * A_GROUP_DIM
CHUNK = 128
B_HEADS = 8
B_HEAD_DIM = 64
B_V_DIM = 2 * B_HEAD_DIM
B_QK_WIDTH = B_HEADS * 2 * B_HEAD_DIM
B_WIDTH = B_HEADS * B_V_DIM
N_IN = 2 * A_WIDTH + 2 * B_QK_WIDTH + B_WIDTH + 2 * D_MODEL
D_FF = 2816
CONV_W = 3
EPS = 1e-6

O_Q = 2 * A_WIDTH
O_K = O_Q + B_QK_WIDTH
O_V = O_K + B_QK_WIDTH
O_G = O_V + B_WIDTH

_MXU_DTYPE = jnp.bfloat16

_V7X_VMEM_BYTES = 64 * 1024 * 1024
_VMEM_LIMIT = _V7X_VMEM_BYTES - 8 * 1024 * 1024

_TM_PRE = 512
_TM_POST = 256
_TQ_CHAIN = 256
_FF_CHUNK = 256
_ONES_ROWS = 16
_HALO = 16


def _rms(x, g):
    return x * lax.rsqrt(jnp.mean(x * x, axis=-1, keepdims=True) + EPS) * g


def _gelu(x):
    return jax.nn.gelu(x, approximate=True)


def _dot(a, b):
    return jnp.dot(a, b, preferred_element_type=jnp.float32)


def _resident(shape, layer):
    nd = len(shape)
    return pl.BlockSpec((None,) + tuple(shape), lambda *_: (layer,) + (0,) * nd,
                        pipeline_mode=pl.Buffered(1))


P_K = 2 * A_WIDTH
P_G = P_K + B_QK_WIDTH
N_TOK = P_G + 2 * D_MODEL
Q_SCALE = (B_HEAD_DIM ** -0.5) * math.log2(math.e)


def _pre_kernel(x_ref, gmix_ref, win_ref, wqvt_ref, bgate_ref, lng_ref, lnb_ref, ws_ref, bst_ref, wpa_ref,
                qt_ref, k_ref, vt_ref, ma_ref, gb_ref, u_sc, vn_sc, ya_sc):
    tm = x_ref.shape[0]
    h = _rms(x_ref[...], gmix_ref[...]).astype(_MXU_DTYPE)

    def proj(lo, width):
        return _dot(h, win_ref[:, lo:lo + width])

    def proj_t(lo, width):
        return lax.dot_general(wqvt_ref[lo:lo + width, :], h, (((1,), (1,)), ((), ())),
                               preferred_element_type=jnp.float32)

    u_sc[...] = _gelu(proj(0, A_WIDTH))
    v = _gelu(proj(A_WIDTH, A_WIDTH))
    mu = jnp.mean(v, axis=-1, keepdims=True)
    vc = v - mu
    var = jnp.mean(vc * vc, axis=-1, keepdims=True)
    vn_sc[...] = (vc * lax.rsqrt(var + EPS) * lng_ref[...] + lnb_ref[...]).astype(_MXU_DTYPE)
    causal = (lax.broadcasted_iota(jnp.int32, (CHUNK, CHUNK), 1)
              <= lax.broadcasted_iota(jnp.int32, (CHUNK, CHUNK), 0))
    for g in range(A_GROUPS):
        cs = slice(g * A_GROUP_DIM, (g + 1) * A_GROUP_DIM)
        w = jnp.where(causal, ws_ref[g], 0.0).astype(_MXU_DTYPE)
        b = bst_ref[:, g:g + 1]
        for c in range(tm // CHUNK):
            rs = slice(c * CHUNK, (c + 1) * CHUNK)
            s = _dot(w, vn_sc[rs, cs]) + b
            ya_sc[rs, cs] = (u_sc[rs, cs] * s).astype(_MXU_DTYPE)

    gate_a = jax.nn.sigmoid(proj(P_G, D_MODEL) + bgate_ref[:, :D_MODEL])
    ma_ref[...] = (gate_a * _dot(ya_sc[...], wpa_ref[...])).astype(ma_ref.dtype)
    gb_ref[...] = jax.nn.sigmoid(proj(P_G + D_MODEL, D_MODEL) + bgate_ref[:, D_MODEL:]).astype(gb_ref.dtype)
    k_ref[...] = proj(P_K, B_QK_WIDTH).astype(k_ref.dtype)
    qt_ref[...] = (proj_t(0, B_QK_WIDTH) * Q_SCALE).astype(qt_ref.dtype)
    vt_ref[...] = proj_t(B_QK_WIDTH, B_WIDTH).astype(vt_ref.dtype)


def _pre_call(x, layer, tm, g_mix, w_tok, w_qvt, b_gate, ln_v_g, ln_v_b, w_s, b_s_t, w_pa):
    t = x.shape[0]
    tok = lambda w: pl.BlockSpec((tm, w), lambda i: (i, 0))
    feat = pl.BlockSpec((None, D_MODEL, tm), lambda i: (i, 0, 0))
    act = jax.ShapeDtypeStruct((t, D_MODEL), _MXU_DTYPE)
    act_t = jax.ShapeDtypeStruct((t // tm, D_MODEL, tm), _MXU_DTYPE)
    return pl.pallas_call(
        _pre_kernel,
        grid=(t // tm,),
        in_specs=[tok(D_MODEL),
                  _resident((1, D_MODEL), layer),
                  _resident((D_MODEL, N_TOK), layer),
                  _resident((B_QK_WIDTH + B_WIDTH, D_MODEL), layer),
                  _resident((1, 2 * D_MODEL), layer),
                  _resident((1, A_WIDTH), layer),
                  _resident((1, A_WIDTH), layer),
                  _resident((A_GROUPS, CHUNK, CHUNK), layer),
                  _resident((CHUNK, A_GROUPS), layer),
                  _resident((A_WIDTH, D_MODEL), layer)],
        out_specs=[feat, tok(D_MODEL), feat, tok(D_MODEL), tok(D_MODEL)],
        out_shape=[act_t, act, act_t, act, act],
        scratch_shapes=[pltpu.VMEM((tm, A_WIDTH), jnp.float32),
                        pltpu.VMEM((tm, A_WIDTH), _MXU_DTYPE),
                        pltpu.VMEM((tm, A_WIDTH), _MXU_DTYPE)],
        compiler_params=pltpu.CompilerParams(dimension_semantics=("parallel",),
                                             vmem_limit_bytes=_VMEM_LIMIT),
        name=f"pre_l{layer}",
    )(x, g_mix, w_tok, w_qvt, b_gate, ln_v_g, ln_v_b, w_s, b_s_t, w_pa)


def _attn_kernel(qt_ref, k_ref, vt_ref, lq1_ref, lk1_ref, lq2_ref, lk2_ref, gsub_ref, o_ref,
                 sa_sc, sb_sc, mt_sc, m_sc, acc_sc, *, lam_init):
    tq = qt_ref.shape[1]
    tk = vt_ref.shape[2]
    tc = min(_TQ_CHAIN, tq)
    i = pl.program_id(2)
    lam = (jnp.exp(jnp.sum(lq1_ref[...] * lk1_ref[...], axis=-1, keepdims=True))
           - jnp.exp(jnp.sum(lq2_ref[...] * lk2_ref[...], axis=-1, keepdims=True)) + lam_init)

    def qcat(c):
        qt = qt_ref[:, c * tc:(c + 1) * tc]
        sub = lax.broadcasted_iota(jnp.int32, qt.shape, 0)
        zero = jnp.zeros_like(qt)
        return jnp.concatenate([jnp.where(sub < B_HEAD_DIM, qt, zero), jnp.where(sub >= B_HEAD_DIM, qt, zero)],
                               axis=1)

    def scores(j, c, q2, masked):
        ks = pl.multiple_of(j * tk, tk)
        s = _dot(k_ref[pl.ds(ks, tk), :], q2)
        if masked:
            kpos = ks + lax.broadcasted_iota(jnp.int32, s.shape, 0)
            qpos = i * tq + c * tc + jnp.bitwise_and(lax.broadcasted_iota(jnp.int32, s.shape, 1), tc - 1)
            s = jnp.where(kpos <= qpos, s, -jnp.inf)
        return s

    n_chain = tq // tc
    qs = [qcat(c) for c in range(n_chain)]
    bufs = (sa_sc, sb_sc)

    def stage(j, b, masked):
        for c in range(n_chain):
            s = scores(j, c, qs[c], masked)
            bufs[b][c] = s
            mt_sc[b, c] = jnp.max(s, axis=0, keepdims=True)

    def fold(j, b):
        v_aug = jnp.concatenate([vt_ref[j], jnp.ones((_ONES_ROWS, tk), _MXU_DTYPE)], axis=0)
        for c in range(n_chain):
            m = m_sc[c]
            m_new = jnp.maximum(m, mt_sc[b, c])
            p = jnp.exp2((bufs[b][c] - m_new).astype(_MXU_DTYPE))
            acc_sc[c] = jnp.exp2(m - m_new) * acc_sc[c] + _dot(v_aug, p)
            m_sc[c] = m_new

    def step(n, b, last):
        if not last:
            stage(n, 1 - b, False)
        fold(jnp.where(n == 0, i, n - 1), b)

    m_sc[...] = jnp.full(m_sc.shape, -jnp.inf, jnp.float32)
    acc_sc[...] = jnp.zeros(acc_sc.shape, jnp.float32)
    stage(i, 0, True)

    def body(u, carry):
        step(2 * u, 0, False)
        step(2 * u + 1, 1, False)
        return carry

    lax.fori_loop(0, i // 2, body, 0)

    @pl.when(i % 2 == 1)
    def _():
        step(i - 1, 0, False)
        step(i, 1, True)

    @pl.when(i % 2 == 0)
    def _():
        step(i, 0, True)

    for c in range(n_chain):
        acc = acc_sc[c]
        o = acc[:B_V_DIM] / acc[B_V_DIM:B_V_DIM + 1]
        o = o[:, :tc] - lam * o[:, tc:]
        o = o * lax.rsqrt(jnp.mean(o * o, axis=0, keepdims=True) + EPS) * gsub_ref[...] * (1.0 - lam_init)
        o_ref[c * tc:(c + 1) * tc, :] = o.T.astype(o_ref.dtype)


def _attn_call(qt, k, vt, layer, lam_init, lam_q1, lam_k1, lam_q2, lam_k2, g_sub_col):
    bsz, s_len, _ = k.shape
    tk = vt.shape[2]
    tq = tk
    tc = min(_TQ_CHAIN, tq)
    n_chain = tq // tc
    assert tq & (tq - 1) == 0
    slabs = s_len // tk
    qspec = pl.BlockSpec((None, B_V_DIM, tq), lambda b, h, i: (b * slabs + i, h, 0))
    kspec = pl.BlockSpec((None, s_len, B_V_DIM), lambda b, h, i: (b, 0, h))
    vspec = pl.BlockSpec((slabs, B_V_DIM, tk), lambda b, h, i: (b, h, 0))
    vec = lambda n: pl.BlockSpec((None, 1, n), lambda b, h, i: (layer, 0, 0))
    return pl.pallas_call(
        functools.partial(_attn_kernel, lam_init=lam_init),
        grid=(bsz, B_HEADS, s_len // tq),
        in_specs=[qspec, kspec, vspec, vec(B_HEAD_DIM), vec(B_HEAD_DIM), vec(B_HEAD_DIM), vec(B_HEAD_DIM),
                  pl.BlockSpec((None, B_V_DIM, 1), lambda b, h, i: (layer, 0, 0))],
        out_specs=pl.BlockSpec((None, tq, B_V_DIM), lambda b, h, i: (b, i, h)),
        out_shape=jax.ShapeDtypeStruct((bsz, s_len, B_WIDTH), _MXU_DTYPE),
        scratch_shapes=[pltpu.VMEM((n_chain, tk, 2 * tc), jnp.float32),
                        pltpu.VMEM((n_chain, tk, 2 * tc), jnp.float32),
                        pltpu.VMEM((2, n_chain, 1, 2 * tc), jnp.float32),
                        pltpu.VMEM((n_chain, 1, 2 * tc), jnp.float32),
                        pltpu.VMEM((n_chain, B_V_DIM + _ONES_ROWS, 2 * tc), jnp.float32)],
        compiler_params=pltpu.CompilerParams(dimension_semantics=("parallel", "parallel", "parallel"),
                                             vmem_limit_bytes=_VMEM_LIMIT),
        name=f"attn_l{layer}",
    )(qt, k, vt, lam_q1, lam_k1, lam_q2, lam_k2, g_sub_col)


def _post_kernel(x_ref, ma_ref, gb_ref, yb_ref, p_ref, wpb_ref, wo_ref, gffn_ref, wup_ref, cw_ref, cb_ref,
                 wdown_ref, gple_ref, wpg_ref, bpg_ref, wple_ref, gfin_ref, o_ref, halo_sc, act_sc,
                 *, tiles_per_seq, final):
    tm = x_ref.shape[0]

    @pl.when(pl.program_id(0) % tiles_per_seq == 0)
    def _():
        halo_sc[...] = jnp.zeros_like(halo_sc)

    pb = _dot(yb_ref[...], wpb_ref[...])
    merged = ma_ref[...].astype(jnp.float32) + gb_ref[...].astype(jnp.float32) * pb
    x1 = x_ref[...] + _dot(merged.astype(_MXU_DTYPE), wo_ref[...])

    h2 = _rms(x1, gffn_ref[...]).astype(_MXU_DTYPE)
    top = lax.broadcasted_iota(jnp.int32, (_HALO, _FF_CHUNK), 0)

    def conv(lo):
        a = _dot(h2, wup_ref[:, lo:lo + _FF_CHUNK])
        w0, w1, w2 = (cw_ref[j:j + 1, lo:lo + _FF_CHUNK] for j in range(CONV_W))
        b = cb_ref[:, lo:lo + _FF_CHUNK]
        y = b + pltpu.roll(a, 2, axis=0) * w0 + pltpu.roll(a, 1, axis=0) * w1 + a * w2
        prev, head = halo_sc[:, lo:lo + _FF_CHUNK], a[:_HALO]
        r2 = jnp.where(top < 2, pltpu.roll(prev, 2, axis=0), pltpu.roll(head, 2, axis=0))
        r1 = jnp.where(top < 1, pltpu.roll(prev, 1, axis=0), pltpu.roll(head, 1, axis=0))
        y_head = b + r2 * w0 + r1 * w1 + head * w2
        halo_sc[:, lo:lo + _FF_CHUNK] = a[tm - _HALO:]
        return y, y_head

    for j in range(D_FF // _FF_CHUNK):
        lo = j * _FF_CHUNK
        yg, yg_head = conv(lo)
        yv, yv_head = conv(D_FF + lo)
        act_sc[:, lo:lo + _FF_CHUNK] = (_gelu(yg) * yv).astype(_MXU_DTYPE)
        act_sc[:_HALO, lo:lo + _FF_CHUNK] = (_gelu(yg_head) * yv_head).astype(_MXU_DTYPE)
    x2 = x1 + _dot(act_sc[...], wdown_ref[...])

    e = _dot(p_ref[...].astype(_MXU_DTYPE), wple_ref[...])
    pg = jax.nn.sigmoid(_dot(_rms(x2, gple_ref[...]).astype(_MXU_DTYPE), wpg_ref[...]) + bpg_ref[...])
    x3 = x2 + pg * e
    if final:
        x3 = _rms(x3, gfin_ref[...])
    o_ref[...] = x3


def _post_call(x, ma, gb, yb, p, layer, s_len, final, w_pb, w_o, g_ffn, w_up, conv_w, conv_b, w_down, g_ple,
               w_pg, b_pg, w_ple, g_final):
    t = x.shape[0]
    tm = min(_TM_POST, s_len)
    tok = lambda w: pl.BlockSpec((tm, w), lambda i: (i, 0))
    return pl.pallas_call(
        functools.partial(_post_kernel, tiles_per_seq=s_len // tm, final=final),
        grid=(t // tm,),
        in_specs=[tok(D_MODEL), tok(D_MODEL), tok(D_MODEL), tok(B_WIDTH),
                  pl.BlockSpec((None, tm, PLE_DIM), lambda i: (layer, i, 0)),
                  _resident((B_WIDTH, D_MODEL), layer),
                  _resident((D_MODEL, D_MODEL), layer),
                  _resident((1, D_MODEL), layer),
                  _resident((D_MODEL, 2 * D_FF), layer),
                  _resident((CONV_W, 2 * D_FF), layer),
                  _resident((1, 2 * D_FF), layer),
                  _resident((D_FF, D_MODEL), layer),
                  _resident((1, D_MODEL), layer),
                  _resident((D_MODEL, D_MODEL), layer),
                  _resident((1, D_MODEL), layer),
                  _resident((PLE_DIM, D_MODEL), layer),
                  pl.BlockSpec((1, D_MODEL), lambda i: (0, 0))],
        out_specs=tok(D_MODEL),
        out_shape=jax.ShapeDtypeStruct((t, D_MODEL), jnp.float32),
        scratch_shapes=[pltpu.VMEM((_HALO, 2 * D_FF), jnp.float32),
                        pltpu.VMEM((tm, D_FF), _MXU_DTYPE)],
        compiler_params=pltpu.CompilerParams(dimension_semantics=("arbitrary",),
                                             vmem_limit_bytes=_VMEM_LIMIT),
        name=f"post_l{layer}",
    )(x, ma, gb, yb, p, w_pb, w_o, g_ffn, w_up, conv_w, conv_b, w_down, g_ple, w_pg, b_pg, w_ple, g_final)


def kernel(x, p, g_mix, w_in, b_gate, ln_v_g, ln_v_b, w_s, b_s, lam_q1, lam_k1, lam_q2, lam_k2, g_sub, w_pa,
           w_pb, w_o, g_ffn, w_up, conv_w, conv_b, w_down, g_ple, w_pg, b_pg, w_ple, g_final):
    bsz, s_len, _ = x.shape
    depth = w_in.shape[0]
    t = bsz * s_len
    mx = lambda w: w.astype(_MXU_DTYPE)
    row = lambda a: a[:, None, :]
    w_tok = mx(jnp.concatenate([w_in[:, :, :O_Q], w_in[:, :, O_K:O_V], w_in[:, :, O_G:]], axis=2))
    w_qvt = mx(jnp.swapaxes(jnp.concatenate([w_in[:, :, O_Q:O_K], w_in[:, :, O_V:O_G]], axis=2), 1, 2))
    w_pa, w_pb, w_o, w_up, w_down, w_pg, w_ple = map(mx, (w_pa, w_pb, w_o, w_up, w_down, w_pg, w_ple))
    g_mix, b_gate, ln_v_g, ln_v_b, g_ffn, conv_b, g_ple, b_pg = map(
        row, (g_mix, b_gate, ln_v_g, ln_v_b, g_ffn, conv_b, g_ple, b_pg))
    lam_q1, lam_k1, lam_q2, lam_k2 = map(row, (lam_q1, lam_k1, lam_q2, lam_k2))
    g_sub_col = g_sub[:, :, None]
    b_s_t = jnp.swapaxes(b_s, 1, 2)
    g_final = g_final[None, :]
    p = p.reshape(depth, t, PLE_DIM)
    x = x.reshape(t, D_MODEL)
    tm_pre = min(_TM_PRE, s_len)
    for i in range(depth):
        lam_init = 0.8 - 0.6 * math.exp(-0.3 * i)
        qt, k, vt, ma, gb = _pre_call(x, i, tm_pre, g_mix, w_tok, w_qvt, b_gate, ln_v_g, ln_v_b, w_s, b_s_t, w_pa)
        yb = _attn_call(qt, k.reshape(bsz, s_len, B_QK_WIDTH), vt, i, lam_init, lam_q1, lam_k1, lam_q2, lam_k2,
                        g_sub_col)
        x = _post_call(x, ma, gb, yb.reshape(t, B_WIDTH), p, i, s_len, i == depth - 1, w_pb, w_o, g_ffn, w_up,
                       conv_w, conv_b, w_down, g_ple, w_pg, b_pg, w_ple, g_final)
    return x.reshape(bsz, s_len, D_MODEL)
```

```python
import functools
import math

import jax
import jax.numpy as jnp
from jax import lax
from jax.experimental import pallas as pl
from jax.experimental.pallas import tpu as pltpu

D_MODEL = 1024
PLE_DIM = 256
A_GROUPS = 8
A_GROUP_DIM = 128
A_WIDTH = A_GROUPS * A_GROUP_DIM
CHUNK = 128
B_HEADS = 8
B_HEAD_DIM = 64
B_V_DIM = 2 * B_HEAD_DIM
B_QK_WIDTH = B_HEADS * 2 * B_HEAD_DIM
B_WIDTH = B_HEADS * B_V_DIM
N_IN = 2 * A_WIDTH + 2 * B_QK_WIDTH + B_WIDTH + 2 * D_MODEL
D_FF = 2816
CONV_W = 3
EPS = 1e-6

O_Q = 2 * A_WIDTH
O_K = O_Q + B_QK_WIDTH
O_V = O_K + B_QK_WIDTH
O_G = O_V + B_WIDTH

_MXU_DTYPE = jnp.bfloat16

_V7X_VMEM_BYTES = 64 * 1024 * 1024
_VMEM_LIMIT = _V7X_VMEM_BYTES - 8 * 1024 * 1024

_TM_PRE = 512
_TM_POST = 256
_TQ_CHAIN = 256
_FF_CHUNK = 256
_ONES_ROWS = 16
_HALO = 16


def _rms(x, g):
    return x * lax.rsqrt(jnp.mean(x * x, axis=-1, keepdims=True) + EPS) * g


def _gelu(x):
    return jax.nn.gelu(x, approximate=True)


def _dot(a, b):
    return jnp.dot(a, b, preferred_element_type=jnp.float32)


def _resident(shape, layer):
    nd = len(shape)
    return pl.BlockSpec((None,) + tuple(shape), lambda *_: (layer,) + (0,) * nd,
                        pipeline_mode=pl.Buffered(1))


P_K = 2 * A_WIDTH
P_G = P_K + B_QK_WIDTH
N_TOK = P_G + 2 * D_MODEL
Q_SCALE = (B_HEAD_DIM ** -0.5) * math.log2(math.e)


def _pre_kernel(x_ref, gmix_ref, win_ref, wqvt_ref, bgate_ref, lng_ref, lnb_ref, ws_ref, bst_ref, wpa_ref,
                qt_ref, k_ref, vt_ref, ma_ref, gb_ref, u_sc, vn_sc, ya_sc):
    tm = x_ref.shape[0]
    h = _rms(x_ref[...], gmix_ref[...]).astype(_MXU_DTYPE)

    def proj(lo, width):
        return _dot(h, win_ref[:, lo:lo + width])

    def proj_t(lo, width):
        return lax.dot_general(wqvt_ref[lo:lo + width, :], h, (((1,), (1,)), ((), ())),
                               preferred_element_type=jnp.float32)

    u_sc[...] = _gelu(proj(0, A_WIDTH))
    v = _gelu(proj(A_WIDTH, A_WIDTH))
    mu = jnp.mean(v, axis=-1, keepdims=True)
    vc = v - mu
    var = jnp.mean(vc * vc, axis=-1, keepdims=True)
    vn_sc[...] = (vc * lax.rsqrt(var + EPS) * lng_ref[...] + lnb_ref[...]).astype(_MXU_DTYPE)
    causal = (lax.broadcasted_iota(jnp.int32, (CHUNK, CHUNK), 1)
              <= lax.broadcasted_iota(jnp.int32, (CHUNK, CHUNK), 0))
    for g in range(A_GROUPS):
        cs = slice(g * A_GROUP_DIM, (g + 1) * A_GROUP_DIM)
        w = jnp.where(causal, ws_ref[g], 0.0).astype(_MXU_DTYPE)
        b = bst_ref[:, g:g + 1]
        for c in range(tm // CHUNK):
            rs = slice(c * CHUNK, (c + 1) * CHUNK)
            s = _dot(w, vn_sc[rs, cs]) + b
            ya_sc[rs, cs] = (u_sc[rs, cs] * s).astype(_MXU_DTYPE)

    gate_a = jax.nn.sigmoid(proj(P_G, D_MODEL) + bgate_ref[:, :D_MODEL])
    ma_ref[...] = (gate_a * _dot(ya_sc[...], wpa_ref[...])).astype(ma_ref.dtype)
    gb_ref[...] = jax.nn.sigmoid(proj(P_G + D_MODEL, D_MODEL) + bgate_ref[:, D_MODEL:]).astype(gb_ref.dtype)
    k_ref[...] = proj(P_K, B_QK_WIDTH).astype(k_ref.dtype)
    qt_ref[...] = (proj_t(0, B_QK_WIDTH) * Q_SCALE).astype(qt_ref.dtype)
    vt_ref[...] = proj_t(B_QK_WIDTH, B_WIDTH).astype(vt_ref.dtype)


def _pre_call(x, layer, tm, g_mix, w_tok, w_qvt, b_gate, ln_v_g, ln_v_b, w_s, b_s_t, w_pa):
    t = x.shape[0]
    tok = lambda w: pl.BlockSpec((tm, w), lambda i: (i, 0))
    feat = pl.BlockSpec((None, D_MODEL, tm), lambda i: (i, 0, 0))
    act = jax.ShapeDtypeStruct((t, D_MODEL), _MXU_DTYPE)
    act_t = jax.ShapeDtypeStruct((t // tm, D_MODEL, tm), _MXU_DTYPE)
    return pl.pallas_call(
        _pre_kernel,
        grid=(t // tm,),
        in_specs=[tok(D_MODEL),
                  _resident((1, D_MODEL), layer),
                  _resident((D_MODEL, N_TOK), layer),
                  _resident((B_QK_WIDTH + B_WIDTH, D_MODEL), layer),
                  _resident((1, 2 * D_MODEL), layer),
                  _resident((1, A_WIDTH), layer),
                  _resident((1, A_WIDTH), layer),
                  _resident((A_GROUPS, CHUNK, CHUNK), layer),
                  _resident((CHUNK, A_GROUPS), layer),
                  _resident((A_WIDTH, D_MODEL), layer)],
        out_specs=[feat, tok(D_MODEL), feat, tok(D_MODEL), tok(D_MODEL)],
        out_shape=[act_t, act, act_t, act, act],
        scratch_shapes=[pltpu.VMEM((tm, A_WIDTH), jnp.float32),
                        pltpu.VMEM((tm, A_WIDTH), _MXU_DTYPE),
                        pltpu.VMEM((tm, A_WIDTH), _MXU_DTYPE)],
        compiler_params=pltpu.CompilerParams(dimension_semantics=("parallel",),
                                             vmem_limit_bytes=_VMEM_LIMIT),
        name=f"pre_l{layer}",
    )(x, g_mix, w_tok, w_qvt, b_gate, ln_v_g, ln_v_b, w_s, b_s_t, w_pa)


def _attn_kernel(qt_ref, k_ref, vt_ref, lq1_ref, lk1_ref, lq2_ref, lk2_ref, gsub_ref, lam0_ref, o_ref,
                 sa_sc, sb_sc, mt_sc, m_sc, acc_sc):
    n_q, _, tq = qt_ref.shape
    tk = vt_ref.shape[2]
    tc = min(_TQ_CHAIN, tq)
    n_chain = tq // tc
    bufs = (sa_sc, sb_sc)
    lam_init = lam0_ref[...]
    lam = (jnp.exp(jnp.sum(lq1_ref[...] * lk1_ref[...], axis=-1, keepdims=True))
           - jnp.exp(jnp.sum(lq2_ref[...] * lk2_ref[...], axis=-1, keepdims=True)) + lam_init)
    ones = jnp.ones((_ONES_ROWS, tk), _MXU_DTYPE)

    def n_keys(c, diagonal):
        return (c + 1) * tc if diagonal else tk

    def stage(i, j, b):
        for c in range(n_chain):
            nk = n_keys(c, i == j)
            qt = qt_ref[i, :, c * tc:(c + 1) * tc]
            sub = lax.broadcasted_iota(jnp.int32, qt.shape, 0)
            zero = jnp.zeros_like(qt)
            q2 = jnp.concatenate([jnp.where(sub < B_HEAD_DIM, qt, zero), jnp.where(sub >= B_HEAD_DIM, qt, zero)],
                                 axis=1)
            s = _dot(k_ref[j * tk:j * tk + nk, :], q2)
            if i == j:
                kpos = lax.broadcasted_iota(jnp.int32, s.shape, 0)
                qpos = c * tc + jnp.bitwise_and(lax.broadcasted_iota(jnp.int32, s.shape, 1), tc - 1)
                s = jnp.where(kpos <= qpos, s, -jnp.inf)
            bufs[b][c, :nk] = s
            mt_sc[b, c] = jnp.max(s, axis=0, keepdims=True)

    def fold(i, j, b):
        for c in range(n_chain):
            nk = n_keys(c, i == j)
            v_aug = jnp.concatenate([vt_ref[j, :, :nk], ones[:, :nk]], axis=0)
            if i == j:
                m_new = mt_sc[b, c]
                acc_sc[c] = _dot(v_aug, jnp.exp2((bufs[b][c, :nk] - m_new).astype(_MXU_DTYPE)))
            else:
                m = m_sc[c]
                m_new = jnp.maximum(m, mt_sc[b, c])
                p = jnp.exp2((bufs[b][c] - m_new).astype(_MXU_DTYPE))
                acc_sc[c] = jnp.exp2(m - m_new) * acc_sc[c] + _dot(v_aug, p)
            m_sc[c] = m_new

    def finish(i):
        for c in range(n_chain):
            acc = acc_sc[c]
            o = acc[:B_V_DIM] / acc[B_V_DIM:B_V_DIM + 1]
            o = o[:, :tc] - lam * o[:, tc:]
            o = o * lax.rsqrt(jnp.mean(o * o, axis=0, keepdims=True) + EPS) * gsub_ref[...] * (1.0 - lam_init)
            o_ref[i * tq + c * tc:i * tq + (c + 1) * tc, :] = o.T.astype(o_ref.dtype)

    pairs = [(i, j) for i in range(n_q) for j in [i] + list(range(i))]
    stage(*pairs[0], 0)
    for n, (i, j) in enumerate(pairs):
        if n + 1 < len(pairs):
            stage(*pairs[n + 1], (n + 1) % 2)
        fold(i, j, n % 2)
        if n + 1 == len(pairs) or pairs[n + 1][0] != i:
            finish(i)


def _attn_call(qt, k, vt, lam_init, lam_q1, lam_k1, lam_q2, lam_k2, g_sub_col):
    bsz, s_len, _ = k.shape
    tk = vt.shape[2]
    tq = tk
    tc = min(_TQ_CHAIN, tq)
    n_chain = tq // tc
    assert tc & (tc - 1) == 0
    slabs = s_len // tk
    kspec = pl.BlockSpec((None, s_len, B_V_DIM), lambda b, h: (b, 0, h))
    vspec = pl.BlockSpec((slabs, B_V_DIM, tk), lambda b, h: (b, h, 0))
    whole = lambda a: pl.BlockSpec(a.shape, lambda b, h: (0,) * a.ndim)
    return pl.pallas_call(
        _attn_kernel,
        grid=(bsz, B_HEADS),
        in_specs=[vspec, kspec, vspec, whole(lam_q1), whole(lam_k1), whole(lam_q2), whole(lam_k2),
                  whole(g_sub_col), whole(lam_init)],
        out_specs=kspec,
        out_shape=jax.ShapeDtypeStruct((bsz, s_len, B_WIDTH), _MXU_DTYPE),
        scratch_shapes=[pltpu.VMEM((n_chain, tk, 2 * tc), jnp.float32),
                        pltpu.VMEM((n_chain, tk, 2 * tc), jnp.float32),
                        pltpu.VMEM((2, n_chain, 1, 2 * tc), jnp.float32),
                        pltpu.VMEM((n_chain, 1, 2 * tc), jnp.float32),
                        pltpu.VMEM((n_chain, B_V_DIM + _ONES_ROWS, 2 * tc), jnp.float32)],
        compiler_params=pltpu.CompilerParams(dimension_semantics=("parallel", "parallel"),
                                             vmem_limit_bytes=_VMEM_LIMIT),
        name="attn",
    )(qt, k, vt, lam_q1, lam_k1, lam_q2, lam_k2, g_sub_col, lam_init)


def _post_kernel(x_ref, ma_ref, gb_ref, yb_ref, p_ref, wpb_ref, wo_ref, gffn_ref, wup_ref, cw_ref, cb_ref,
                 wdown_ref, gple_ref, wpg_ref, bpg_ref, wple_ref, gfin_ref, o_ref, halo_sc, act_sc,
                 *, tiles_per_seq, final):
    tm = x_ref.shape[0]

    @pl.when(pl.program_id(0) % tiles_per_seq == 0)
    def _():
        halo_sc[...] = jnp.zeros_like(halo_sc)

    pb = _dot(yb_ref[...], wpb_ref[...])
    merged = ma_ref[...].astype(jnp.float32) + gb_ref[...].astype(jnp.float32) * pb
    x1 = x_ref[...] + _dot(merged.astype(_MXU_DTYPE), wo_ref[...])

    h2 = _rms(x1, gffn_ref[...]).astype(_MXU_DTYPE)
    top = lax.broadcasted_iota(jnp.int32, (_HALO, _FF_CHUNK), 0)

    def conv(lo):
        a = _dot(h2, wup_ref[:, lo:lo + _FF_CHUNK])
        w0, w1, w2 = (cw_ref[j:j + 1, lo:lo + _FF_CHUNK] for j in range(CONV_W))
        b = cb_ref[:, lo:lo + _FF_CHUNK]
        y = b + pltpu.roll(a, 2, axis=0) * w0 + pltpu.roll(a, 1, axis=0) * w1 + a * w2
        prev, head = halo_sc[:, lo:lo + _FF_CHUNK], a[:_HALO]
        r2 = jnp.where(top < 2, pltpu.roll(prev, 2, axis=0), pltpu.roll(head, 2, axis=0))
        r1 = jnp.where(top < 1, pltpu.roll(prev, 1, axis=0), pltpu.roll(head, 1, axis=0))
        y_head = b + r2 * w0 + r1 * w1 + head * w2
        halo_sc[:, lo:lo + _FF_CHUNK] = a[tm - _HALO:]
        return y, y_head

    for j in range(D_FF // _FF_CHUNK):
        lo = j * _FF_CHUNK
        yg, yg_head = conv(lo)
        yv, yv_head = conv(D_FF + lo)
        act_sc[:, lo:lo + _FF_CHUNK] = (_gelu(yg) * yv).astype(_MXU_DTYPE)
        act_sc[:_HALO, lo:lo + _FF_CHUNK] = (_gelu(yg_head) * yv_head).astype(_MXU_DTYPE)
    x2 = x1 + _dot(act_sc[...], wdown_ref[...])

    e = _dot(p_ref[...].astype(_MXU_DTYPE), wple_ref[...])
    pg = jax.nn.sigmoid(_dot(_rms(x2, gple_ref[...]).astype(_MXU_DTYPE), wpg_ref[...]) + bpg_ref[...])
    x3 = x2 + pg * e
    if final:
        x3 = _rms(x3, gfin_ref[...])
    o_ref[...] = x3


def _post_call(x, ma, gb, yb, p, layer, s_len, final, w_pb, w_o, g_ffn, w_up, conv_w, conv_b, w_down, g_ple,
               w_pg, b_pg, w_ple, g_final):
    t = x.shape[0]
    tm = min(_TM_POST, s_len)
    tok = lambda w: pl.BlockSpec((tm, w), lambda i: (i, 0))
    return pl.pallas_call(
        functools.partial(_post_kernel, tiles_per_seq=s_len // tm, final=final),
        grid=(t // tm,),
        in_specs=[tok(D_MODEL), tok(D_MODEL), tok(D_MODEL), tok(B_WIDTH),
                  pl.BlockSpec((None, tm, PLE_DIM), lambda i: (layer, i, 0)),
                  _resident((B_WIDTH, D_MODEL), layer),
                  _resident((D_MODEL, D_MODEL), layer),
                  _resident((1, D_MODEL), layer),
                  _resident((D_MODEL, 2 * D_FF), layer),
                  _resident((CONV_W, 2 * D_FF), layer),
                  _resident((1, 2 * D_FF), layer),
                  _resident((D_FF, D_MODEL), layer),
                  _resident((1, D_MODEL), layer),
                  _resident((D_MODEL, D_MODEL), layer),
                  _resident((1, D_MODEL), layer),
                  _resident((PLE_DIM, D_MODEL), layer),
                  pl.BlockSpec((1, D_MODEL), lambda i: (0, 0))],
        out_specs=tok(D_MODEL),
        out_shape=jax.ShapeDtypeStruct((t, D_MODEL), jnp.float32),
        scratch_shapes=[pltpu.VMEM((_HALO, 2 * D_FF), jnp.float32),
                        pltpu.VMEM((tm, D_FF), _MXU_DTYPE)],
        compiler_params=pltpu.CompilerParams(dimension_semantics=("arbitrary",),
                                             vmem_limit_bytes=_VMEM_LIMIT),
        name=f"post_l{layer}",
    )(x, ma, gb, yb, p, w_pb, w_o, g_ffn, w_up, conv_w, conv_b, w_down, g_ple, w_pg, b_pg, w_ple, g_final)


def kernel(x, p, g_mix, w_in, b_gate, ln_v_g, ln_v_b, w_s, b_s, lam_q1, lam_k1, lam_q2, lam_k2, g_sub, w_pa,
           w_pb, w_o, g_ffn, w_up, conv_w, conv_b, w_down, g_ple, w_pg, b_pg, w_ple, g_final):
    bsz, s_len, _ = x.shape
    depth = w_in.shape[0]
    t = bsz * s_len
    mx = lambda w: w.astype(_MXU_DTYPE)
    row = lambda a: a[:, None, :]
    w_tok = mx(jnp.concatenate([w_in[:, :, :O_Q], w_in[:, :, O_K:O_V], w_in[:, :, O_G:]], axis=2))
    w_qvt = mx(jnp.swapaxes(jnp.concatenate([w_in[:, :, O_Q:O_K], w_in[:, :, O_V:O_G]], axis=2), 1, 2))
    w_pa, w_pb, w_o, w_up, w_down, w_pg, w_ple = map(mx, (w_pa, w_pb, w_o, w_up, w_down, w_pg, w_ple))
    g_mix, b_gate, ln_v_g, ln_v_b, g_ffn, conv_b, g_ple, b_pg = map(
        row, (g_mix, b_gate, ln_v_g, ln_v_b, g_ffn, conv_b, g_ple, b_pg))
    lam_q1, lam_k1, lam_q2, lam_k2 = map(row, (lam_q1, lam_k1, lam_q2, lam_k2))
    g_sub_col = g_sub[:, :, None]
    b_s_t = jnp.swapaxes(b_s, 1, 2)
    g_final = g_final[None, :]
    p = p.reshape(depth, t, PLE_DIM)
    x = x.reshape(t, D_MODEL)
    tm_pre = min(_TM_PRE, s_len)
    for i in range(depth):
        lam_init = 0.8 - 0.6 * math.exp(-0.3 * i)
        qt, k, vt, ma, gb = _pre_call(x, i, tm_pre, g_mix, w_tok, w_qvt, b_gate, ln_v_g, ln_v_b, w_s, b_s_t, w_pa)
        yb = _attn_call(qt, k.reshape(bsz, s_len, B_QK_WIDTH), vt, jnp.full((1, 1), lam_init, jnp.float32),
                        lam_q1[i], lam_k1[i], lam_q2[i], lam_k2[i], g_sub_col[i])
        x = _post_call(x, ma, gb, yb.reshape(t, B_WIDTH), p, i, s_len, i == depth - 1, w_pb, w_o, g_ffn, w_up,
                       conv_w, conv_b, w_down, g_ple, w_pg, b_pg, w_ple, g_final)
    return x.reshape(bsz, s_len, D_MODEL)
```

```python
import functools
import math

import jax
import jax.numpy as jnp
from jax import lax
from jax.experimental import pallas as pl
from jax.experimental.pallas import tpu as pltpu

D_MODEL = 1024
PLE_DIM = 256
A_GROUPS = 8
A_GROUP_DIM = 128
A_WIDTH = A_GROUPS * A_GROUP_DIM
CHUNK = 128
B_HEADS = 8
B_HEAD_DIM = 64
B_V_DIM = 2 * B_HEAD_DIM
B_QK_WIDTH = B_HEADS * 2 * B_HEAD_DIM
B_WIDTH = B_HEADS * B_V_DIM
N_IN = 2 * A_WIDTH + 2 * B_QK_WIDTH + B_WIDTH + 2 * D_MODEL
D_FF = 2816
CONV_W = 3
EPS = 1e-6

O_Q = 2 * A_WIDTH
O_K = O_Q + B_QK_WIDTH
O_V = O_K + B_QK_WIDTH
O_G = O_V + B_WIDTH

_MXU_DTYPE = jnp.bfloat16

_V7X_VMEM_BYTES = 64 * 1024 * 1024
_VMEM_LIMIT = _V7X_VMEM_BYTES - 8 * 1024 * 1024

_TM_PRE = 512
_TM_POST = 256
_TQ_CHAIN = 128
_FF_CHUNK = 256
_ONES_ROWS = 16
_HALO = 16


def _rms(x, g):
    return x * lax.rsqrt(jnp.mean(x * x, axis=-1, keepdims=True) + EPS) * g


def _gelu(x):
    return jax.nn.gelu(x, approximate=True)


def _dot(a, b):
    return jnp.dot(a, b, preferred_element_type=jnp.float32)


def _resident(shape, layer):
    nd = len(shape)
    return pl.BlockSpec((None,) + tuple(shape), lambda *_: (layer,) + (0,) * nd,
                        pipeline_mode=pl.Buffered(1))


P_K = 2 * A_WIDTH
P_G = P_K + B_QK_WIDTH
N_TOK = P_G + 2 * D_MODEL
Q_SCALE = (B_HEAD_DIM ** -0.5) * math.log2(math.e)


def _pre_kernel(x_ref, gmix_ref, win_ref, wqvt_ref, bgate_ref, lng_ref, lnb_ref, ws_ref, bst_ref, wpa_ref,
                qt_ref, k_ref, vt_ref, ma_ref, gb_ref, u_sc, vn_sc, ya_sc, ga_sc):
    tm = x_ref.shape[0]
    h = _rms(x_ref[...], gmix_ref[...]).astype(_MXU_DTYPE)

    def proj(lo, width):
        return _dot(h, win_ref[:, lo:lo + width])

    def proj_t(lo, width):
        return lax.dot_general(wqvt_ref[lo:lo + width, :], h, (((1,), (1,)), ((), ())),
                               preferred_element_type=jnp.float32)

    zu = proj(0, A_WIDTH)
    zv = proj(A_WIDTH, A_WIDTH)
    k_ref[...] = proj(P_K, B_QK_WIDTH).astype(k_ref.dtype)
    qt_ref[...] = (proj_t(0, B_QK_WIDTH) * Q_SCALE).astype(qt_ref.dtype)
    vt_ref[...] = proj_t(B_QK_WIDTH, B_WIDTH).astype(vt_ref.dtype)
    gb_ref[...] = jax.nn.sigmoid(proj(P_G + D_MODEL, D_MODEL) + bgate_ref[:, D_MODEL:]).astype(gb_ref.dtype)
    ga_sc[...] = jax.nn.sigmoid(proj(P_G, D_MODEL) + bgate_ref[:, :D_MODEL])

    u_sc[...] = _gelu(zu)
    v = _gelu(zv)
    mu = jnp.mean(v, axis=-1, keepdims=True)
    vc = v - mu
    var = jnp.mean(vc * vc, axis=-1, keepdims=True)
    vn_sc[...] = (vc * lax.rsqrt(var + EPS) * lng_ref[...] + lnb_ref[...]).astype(_MXU_DTYPE)
    causal = (lax.broadcasted_iota(jnp.int32, (CHUNK, CHUNK), 1)
              <= lax.broadcasted_iota(jnp.int32, (CHUNK, CHUNK), 0))
    for g in range(A_GROUPS):
        cs = slice(g * A_GROUP_DIM, (g + 1) * A_GROUP_DIM)
        w = jnp.where(causal, ws_ref[g], 0.0).astype(_MXU_DTYPE)
        b = bst_ref[:, g:g + 1]
        for c in range(tm // CHUNK):
            rs = slice(c * CHUNK, (c + 1) * CHUNK)
            s = _dot(w, vn_sc[rs, cs]) + b
            ya_sc[rs, cs] = (u_sc[rs, cs] * s).astype(_MXU_DTYPE)

    ma_ref[...] = (ga_sc[...] * _dot(ya_sc[...], wpa_ref[...])).astype(ma_ref.dtype)


def _pre_call(x, layer, tm, g_mix, w_tok, w_qvt, b_gate, ln_v_g, ln_v_b, w_s, b_s_t, w_pa):
    t = x.shape[0]
    tok = lambda w: pl.BlockSpec((tm, w), lambda i: (i, 0))
    feat = pl.BlockSpec((None, D_MODEL, tm), lambda i: (i, 0, 0))
    act = jax.ShapeDtypeStruct((t, D_MODEL), _MXU_DTYPE)
    act_t = jax.ShapeDtypeStruct((t // tm, D_MODEL, tm), _MXU_DTYPE)
    return pl.pallas_call(
        _pre_kernel,
        grid=(t // tm,),
        in_specs=[tok(D_MODEL),
                  _resident((1, D_MODEL), layer),
                  _resident((D_MODEL, N_TOK), layer),
                  _resident((B_QK_WIDTH + B_WIDTH, D_MODEL), layer),
                  _resident((1, 2 * D_MODEL), layer),
                  _resident((1, A_WIDTH), layer),
                  _resident((1, A_WIDTH), layer),
                  _resident((A_GROUPS, CHUNK, CHUNK), layer),
                  _resident((CHUNK, A_GROUPS), layer),
                  _resident((A_WIDTH, D_MODEL), layer)],
        out_specs=[feat, tok(D_MODEL), feat, tok(D_MODEL), tok(D_MODEL)],
        out_shape=[act_t, act, act_t, act, act],
        scratch_shapes=[pltpu.VMEM((tm, A_WIDTH), jnp.float32),
                        pltpu.VMEM((tm, A_WIDTH), _MXU_DTYPE),
                        pltpu.VMEM((tm, A_WIDTH), _MXU_DTYPE),
                        pltpu.VMEM((tm, D_MODEL), jnp.float32)],
        compiler_params=pltpu.CompilerParams(dimension_semantics=("parallel",),
                                             vmem_limit_bytes=_VMEM_LIMIT),
        name=f"pre_l{layer}",
    )(x, g_mix, w_tok, w_qvt, b_gate, ln_v_g, ln_v_b, w_s, b_s_t, w_pa)


def _attn_kernel(qt_ref, k_ref, vt_ref, lq1_ref, lk1_ref, lq2_ref, lk2_ref, gsub_ref, lam0_ref, o_ref,
                 sa_sc, sb_sc, mt_sc, m_sc, acc_sc):
    n_q, _, tq = qt_ref.shape
    tk = vt_ref.shape[2]
    tc = min(_TQ_CHAIN, tq)
    n_chain = tq // tc
    bufs = (sa_sc, sb_sc)
    lam_init = lam0_ref[...]
    lam = (jnp.exp(jnp.sum(lq1_ref[...] * lk1_ref[...], axis=-1, keepdims=True))
           - jnp.exp(jnp.sum(lq2_ref[...] * lk2_ref[...], axis=-1, keepdims=True)) + lam_init)
    ones = jnp.ones((_ONES_ROWS, tk), _MXU_DTYPE)

    def n_keys(c, diagonal):
        return (c + 1) * tc if diagonal else tk

    def stage(i, j, b):
        for c in range(n_chain):
            nk = n_keys(c, i == j)
            qt = qt_ref[i, :, c * tc:(c + 1) * tc]
            sub = lax.broadcasted_iota(jnp.int32, qt.shape, 0)
            zero = jnp.zeros_like(qt)
            q2 = jnp.concatenate([jnp.where(sub < B_HEAD_DIM, qt, zero), jnp.where(sub >= B_HEAD_DIM, qt, zero)],
                                 axis=1)
            s = _dot(k_ref[j * tk:j * tk + nk, :], q2)
            if i == j:
                kpos = lax.broadcasted_iota(jnp.int32, s.shape, 0)
                qpos = c * tc + jnp.bitwise_and(lax.broadcasted_iota(jnp.int32, s.shape, 1), tc - 1)
                s = jnp.where(kpos <= qpos, s, -jnp.inf)
            bufs[b][c, :nk] = s
            mt_sc[b, c] = jnp.max(s, axis=0, keepdims=True)

    def fold(i, j, b):
        for c in range(n_chain):
            nk = n_keys(c, i == j)
            v_aug = jnp.concatenate([vt_ref[j, :, :nk], ones[:, :nk]], axis=0)
            if i == j:
                m_new = mt_sc[b, c]
                acc_sc[c] = _dot(v_aug, jnp.exp2((bufs[b][c, :nk] - m_new).astype(_MXU_DTYPE)))
            else:
                m = m_sc[c]
                m_new = jnp.maximum(m, mt_sc[b, c])
                p = jnp.exp2((bufs[b][c] - m_new).astype(_MXU_DTYPE))
                acc_sc[c] = jnp.exp2(m - m_new) * acc_sc[c] + _dot(v_aug, p)
            m_sc[c] = m_new

    def finish(i):
        for c in range(n_chain):
            acc = acc_sc[c]
            o = acc[:B_V_DIM] / acc[B_V_DIM:B_V_DIM + 1]
            o = o[:, :tc] - lam * o[:, tc:]
            o = o * lax.rsqrt(jnp.mean(o * o, axis=0, keepdims=True) + EPS) * gsub_ref[...] * (1.0 - lam_init)
            o_ref[i * tq + c * tc:i * tq + (c + 1) * tc, :] = o.T.astype(o_ref.dtype)

    pairs = [(i, j) for i in range(n_q) for j in [i] + list(range(i))]
    stage(*pairs[0], 0)
    for n, (i, j) in enumerate(pairs):
        if n + 1 < len(pairs):
            stage(*pairs[n + 1], (n + 1) % 2)
        fold(i, j, n % 2)
        if n + 1 == len(pairs) or pairs[n + 1][0] != i:
            finish(i)


def _attn_call(qt, k, vt, lam_init, lam_q1, lam_k1, lam_q2, lam_k2, g_sub_col):
    bsz, s_len, _ = k.shape
    tk = vt.shape[2]
    tq = tk
    tc = min(_TQ_CHAIN, tq)
    n_chain = tq // tc
    assert tc & (tc - 1) == 0
    slabs = s_len // tk
    kspec = pl.BlockSpec((None, s_len, B_V_DIM), lambda b, h: (b, 0, h))
    vspec = pl.BlockSpec((slabs, B_V_DIM, tk), lambda b, h: (b, h, 0))
    whole = lambda a: pl.BlockSpec(a.shape, lambda b, h: (0,) * a.ndim)
    return pl.pallas_call(
        _attn_kernel,
        grid=(bsz, B_HEADS),
        in_specs=[vspec, kspec, vspec, whole(lam_q1), whole(lam_k1), whole(lam_q2), whole(lam_k2),
                  whole(g_sub_col), whole(lam_init)],
        out_specs=kspec,
        out_shape=jax.ShapeDtypeStruct((bsz, s_len, B_WIDTH), _MXU_DTYPE),
        scratch_shapes=[pltpu.VMEM((n_chain, tk, 2 * tc), jnp.float32),
                        pltpu.VMEM((n_chain, tk, 2 * tc), jnp.float32),
                        pltpu.VMEM((2, n_chain, 1, 2 * tc), jnp.float32),
                        pltpu.VMEM((n_chain, 1, 2 * tc), jnp.float32),
                        pltpu.VMEM((n_chain, B_V_DIM + _ONES_ROWS, 2 * tc), jnp.float32)],
        compiler_params=pltpu.CompilerParams(dimension_semantics=("parallel", "parallel"),
                                             vmem_limit_bytes=_VMEM_LIMIT),
        name="attn",
    )(qt, k, vt, lam_q1, lam_k1, lam_q2, lam_k2, g_sub_col, lam_init)


def _post_kernel(x_ref, ma_ref, gb_ref, yb_ref, p_ref, wpb_ref, wo_ref, gffn_ref, wup_ref, cw_ref, cb_ref,
                 wdown_ref, gple_ref, wpg_ref, bpg_ref, wple_ref, gfin_ref, o_ref, halo_sc, act_sc,
                 *, tiles_per_seq, final):
    tm = x_ref.shape[0]

    @pl.when(pl.program_id(0) % tiles_per_seq == 0)
    def _():
        halo_sc[...] = jnp.zeros_like(halo_sc)

    pb = _dot(yb_ref[...], wpb_ref[...])
    merged = ma_ref[...].astype(jnp.float32) + gb_ref[...].astype(jnp.float32) * pb
    x1 = x_ref[...] + _dot(merged.astype(_MXU_DTYPE), wo_ref[...])

    h2 = _rms(x1, gffn_ref[...]).astype(_MXU_DTYPE)
    top = lax.broadcasted_iota(jnp.int32, (_HALO, _FF_CHUNK), 0)

    def conv(lo):
        a = _dot(h2, wup_ref[:, lo:lo + _FF_CHUNK])
        w0, w1, w2 = (cw_ref[j:j + 1, lo:lo + _FF_CHUNK] for j in range(CONV_W))
        b = cb_ref[:, lo:lo + _FF_CHUNK]
        y = b + pltpu.roll(a, 2, axis=0) * w0 + pltpu.roll(a, 1, axis=0) * w1 + a * w2
        prev, head = halo_sc[:, lo:lo + _FF_CHUNK], a[:_HALO]
        r2 = jnp.where(top < 2, pltpu.roll(prev, 2, axis=0), pltpu.roll(head, 2, axis=0))
        r1 = jnp.where(top < 1, pltpu.roll(prev, 1, axis=0), pltpu.roll(head, 1, axis=0))
        y_head = b + r2 * w0 + r1 * w1 + head * w2
        halo_sc[:, lo:lo + _FF_CHUNK] = a[tm - _HALO:]
        return y, y_head

    for j in range(D_FF // _FF_CHUNK):
        lo = j * _FF_CHUNK
        yg, yg_head = conv(lo)
        yv, yv_head = conv(D_FF + lo)
        act_sc[:, lo:lo + _FF_CHUNK] = (_gelu(yg) * yv).astype(_MXU_DTYPE)
        act_sc[:_HALO, lo:lo + _FF_CHUNK] = (_gelu(yg_head) * yv_head).astype(_MXU_DTYPE)
    x2 = x1 + _dot(act_sc[...], wdown_ref[...])

    e = _dot(p_ref[...].astype(_MXU_DTYPE), wple_ref[...])
    pg = jax.nn.sigmoid(_dot(_rms(x2, gple_ref[...]).astype(_MXU_DTYPE), wpg_ref[...]) + bpg_ref[...])
    x3 = x2 + pg * e
    if final:
        x3 = _rms(x3, gfin_ref[...])
    o_ref[...] = x3


def _post_call(x, ma, gb, yb, p, layer, s_len, final, w_pb, w_o, g_ffn, w_up, conv_w, conv_b, w_down, g_ple,
               w_pg, b_pg, w_ple, g_final):
    t = x.shape[0]
    tm = min(_TM_POST, s_len)
    tok = lambda w: pl.BlockSpec((tm, w), lambda i: (i, 0))
    return pl.pallas_call(
        functools.partial(_post_kernel, tiles_per_seq=s_len // tm, final=final),
        grid=(t // tm,),
        in_specs=[tok(D_MODEL), tok(D_MODEL), tok(D_MODEL), tok(B_WIDTH),
                  pl.BlockSpec((None, tm, PLE_DIM), lambda i: (layer, i, 0)),
                  _resident((B_WIDTH, D_MODEL), layer),
                  _resident((D_MODEL, D_MODEL), layer),
                  _resident((1, D_MODEL), layer),
                  _resident((D_MODEL, 2 * D_FF), layer),
                  _resident((CONV_W, 2 * D_FF), layer),
                  _resident((1, 2 * D_FF), layer),
                  _resident((D_FF, D_MODEL), layer),
                  _resident((1, D_MODEL), layer),
                  _resident((D_MODEL, D_MODEL), layer),
                  _resident((1, D_MODEL), layer),
                  _resident((PLE_DIM, D_MODEL), layer),
                  pl.BlockSpec((1, D_MODEL), lambda i: (0, 0))],
        out_specs=tok(D_MODEL),
        out_shape=jax.ShapeDtypeStruct((t, D_MODEL), jnp.float32),
        scratch_shapes=[pltpu.VMEM((_HALO, 2 * D_FF), jnp.float32),
                        pltpu.VMEM((tm, D_FF), _MXU_DTYPE)],
        compiler_params=pltpu.CompilerParams(dimension_semantics=("arbitrary",),
                                             vmem_limit_bytes=_VMEM_LIMIT),
        name=f"post_l{layer}",
    )(x, ma, gb, yb, p, w_pb, w_o, g_ffn, w_up, conv_w, conv_b, w_down, g_ple, w_pg, b_pg, w_ple, g_final)


def kernel(x, p, g_mix, w_in, b_gate, ln_v_g, ln_v_b, w_s, b_s, lam_q1, lam_k1, lam_q2, lam_k2, g_sub, w_pa,
           w_pb, w_o, g_ffn, w_up, conv_w, conv_b, w_down, g_ple, w_pg, b_pg, w_ple, g_final):
    bsz, s_len, _ = x.shape
    depth = w_in.shape[0]
    t = bsz * s_len
    mx = lambda w: w.astype(_MXU_DTYPE)
    row = lambda a: a[:, None, :]
    w_tok = mx(jnp.concatenate([w_in[:, :, :O_Q], w_in[:, :, O_K:O_V], w_in[:, :, O_G:]], axis=2))
    w_qvt = mx(jnp.swapaxes(jnp.concatenate([w_in[:, :, O_Q:O_K], w_in[:, :, O_V:O_G]], axis=2), 1, 2))
    w_pa, w_pb, w_o, w_up, w_down, w_pg, w_ple = map(mx, (w_pa, w_pb, w_o, w_up, w_down, w_pg, w_ple))
    g_mix, b_gate, ln_v_g, ln_v_b, g_ffn, conv_b, g_ple, b_pg = map(
        row, (g_mix, b_gate, ln_v_g, ln_v_b, g_ffn, conv_b, g_ple, b_pg))
    lam_q1, lam_k1, lam_q2, lam_k2 = map(row, (lam_q1, lam_k1, lam_q2, lam_k2))
    g_sub_col = g_sub[:, :, None]
    b_s_t = jnp.swapaxes(b_s, 1, 2)
    g_final = g_final[None, :]
    p = p.reshape(depth, t, PLE_DIM)
    x = x.reshape(t, D_MODEL)
    tm_pre = min(_TM_PRE, s_len)
    for i in range(depth):
        lam_init = 0.8 - 0.6 * math.exp(-0.3 * i)
        qt, k, vt, ma, gb = _pre_call(x, i, tm_pre, g_mix, w_tok, w_qvt, b_gate, ln_v_g, ln_v_b, w_s, b_s_t, w_pa)
        yb = _attn_call(qt, k.reshape(bsz, s_len, B_QK_WIDTH), vt, jnp.full((1, 1), lam_init, jnp.float32),
                        lam_q1[i], lam_k1[i], lam_q2[i], lam_k2[i], g_sub_col[i])
        x = _post_call(x, ma, gb, yb.reshape(t, B_WIDTH), p, i, s_len, i == depth - 1, w_pb, w_o, g_ffn, w_up,
                       conv_w, conv_b, w_down, g_ple, w_pg, b_pg, w_ple, g_final)
    return x.reshape(bsz, s_len, D_MODEL)
```

```python
import functools
import math

import jax
import jax.numpy as jnp
from jax import lax
from jax.experimental import pallas as pl
from jax.experimental.pallas import tpu as pltpu

D_MODEL = 1024
PLE_DIM = 256
A_GROUPS = 8
A_GROUP_DIM = 128
A_WIDTH = A_GROUPS * A_GROUP_DIM
CHUNK = 128
B_HEADS = 8
B_HEAD_DIM = 64
B_V_DIM = 2 * B_HEAD_DIM
B_QK_WIDTH = B_HEADS * 2 * B_HEAD_DIM
B_WIDTH = B_HEADS * B_V_DIM
N_IN = 2 * A_WIDTH + 2 * B_QK_WIDTH + B_WIDTH + 2 * D_MODEL
D_FF = 2816
CONV_W = 3
EPS = 1e-6

O_Q = 2 * A_WIDTH
O_K = O_Q + B_QK_WIDTH
O_V = O_K + B_QK_WIDTH
O_G = O_V + B_WIDTH

_MXU_DTYPE = jnp.bfloat16

_V7X_VMEM_BYTES = 64 * 1024 * 1024
_VMEM_LIMIT = _V7X_VMEM_BYTES - 8 * 1024 * 1024

_TM_PRE = 512
_TM_POST = 256
_TQ_CHAIN = 128
_FF_CHUNK = 256
_ONES_ROWS = 16
_HALO = 16


def _rms(x, g):
    return x * lax.rsqrt(jnp.mean(x * x, axis=-1, keepdims=True) + EPS) * g


def _gelu(x):
    return jax.nn.gelu(x, approximate=True)


def _dot(a, b):
    return jnp.dot(a, b, preferred_element_type=jnp.float32)


def _resident(shape, layer, col_block=0):
    nd = len(shape)
    return pl.BlockSpec((None,) + tuple(shape), lambda *_: (layer,) + (0,) * (nd - 1) + (col_block,),
                        pipeline_mode=pl.Buffered(1))


Q_SCALE = (B_HEAD_DIM ** -0.5) * math.log2(math.e)


def _pre_kernel(x_ref, gmix_ref, wuv_ref, wk_ref, wga_ref, wgb_ref, wqvt_ref, bgate_ref, lng_ref, lnb_ref, ws_ref,
                bst_ref, wpa_ref, qt_ref, k_ref, vt_ref, ma_ref, gb_ref, u_sc, vn_sc, ya_sc, ga_sc):
    tm = x_ref.shape[0]
    h = _rms(x_ref[...], gmix_ref[...]).astype(_MXU_DTYPE)

    def proj_t(lo, width):
        return lax.dot_general(wqvt_ref[lo:lo + width, :], h, (((1,), (1,)), ((), ())),
                               preferred_element_type=jnp.float32)

    zu = _dot(h, wuv_ref[:, :A_WIDTH])
    zv = _dot(h, wuv_ref[:, A_WIDTH:])
    k_ref[...] = _dot(h, wk_ref[...]).astype(k_ref.dtype)
    qt_ref[...] = (proj_t(0, B_QK_WIDTH) * Q_SCALE).astype(qt_ref.dtype)
    vt_ref[...] = proj_t(B_QK_WIDTH, B_WIDTH).astype(vt_ref.dtype)
    gb_ref[...] = jax.nn.sigmoid(_dot(h, wgb_ref[...]) + bgate_ref[:, D_MODEL:]).astype(gb_ref.dtype)
    ga_sc[...] = jax.nn.sigmoid(_dot(h, wga_ref[...]) + bgate_ref[:, :D_MODEL])

    u_sc[...] = _gelu(zu)
    v = _gelu(zv)
    mu = jnp.mean(v, axis=-1, keepdims=True)
    vc = v - mu
    var = jnp.mean(vc * vc, axis=-1, keepdims=True)
    vn_sc[...] = (vc * lax.rsqrt(var + EPS) * lng_ref[...] + lnb_ref[...]).astype(_MXU_DTYPE)
    causal = (lax.broadcasted_iota(jnp.int32, (CHUNK, CHUNK), 1)
              <= lax.broadcasted_iota(jnp.int32, (CHUNK, CHUNK), 0))
    for g in range(A_GROUPS):
        cs = slice(g * A_GROUP_DIM, (g + 1) * A_GROUP_DIM)
        w = jnp.where(causal, ws_ref[g], 0.0).astype(_MXU_DTYPE)
        b = bst_ref[:, g:g + 1]
        for c in range(tm // CHUNK):
            rs = slice(c * CHUNK, (c + 1) * CHUNK)
            s = _dot(w, vn_sc[rs, cs]) + b
            ya_sc[rs, cs] = (u_sc[rs, cs] * s).astype(_MXU_DTYPE)

    ma_ref[...] = (ga_sc[...] * _dot(ya_sc[...], wpa_ref[...])).astype(ma_ref.dtype)


def _pre_call(x, layer, tm, g_mix, w_in, w_qvt, b_gate, ln_v_g, ln_v_b, w_s, b_s_t, w_pa):
    t = x.shape[0]
    tok = lambda w: pl.BlockSpec((tm, w), lambda i: (i, 0))
    feat = pl.BlockSpec((None, D_MODEL, tm), lambda i: (i, 0, 0))
    act = jax.ShapeDtypeStruct((t, D_MODEL), _MXU_DTYPE)
    act_t = jax.ShapeDtypeStruct((t // tm, D_MODEL, tm), _MXU_DTYPE)
    return pl.pallas_call(
        _pre_kernel,
        grid=(t // tm,),
        in_specs=[tok(D_MODEL),
                  _resident((1, D_MODEL), layer),
                  _resident((D_MODEL, 2 * A_WIDTH), layer, 0),
                  _resident((D_MODEL, B_QK_WIDTH), layer, O_K // B_QK_WIDTH),
                  _resident((D_MODEL, D_MODEL), layer, O_G // D_MODEL),
                  _resident((D_MODEL, D_MODEL), layer, O_G // D_MODEL + 1),
                  _resident((B_QK_WIDTH + B_WIDTH, D_MODEL), layer),
                  _resident((1, 2 * D_MODEL), layer),
                  _resident((1, A_WIDTH), layer),
                  _resident((1, A_WIDTH), layer),
                  _resident((A_GROUPS, CHUNK, CHUNK), layer),
                  _resident((CHUNK, A_GROUPS), layer),
                  _resident((A_WIDTH, D_MODEL), layer)],
        out_specs=[feat, tok(D_MODEL), feat, tok(D_MODEL), tok(D_MODEL)],
        out_shape=[act_t, act, act_t, act, act],
        scratch_shapes=[pltpu.VMEM((tm, A_WIDTH), jnp.float32),
                        pltpu.VMEM((tm, A_WIDTH), _MXU_DTYPE),
                        pltpu.VMEM((tm, A_WIDTH), _MXU_DTYPE),
                        pltpu.VMEM((tm, D_MODEL), jnp.float32)],
        compiler_params=pltpu.CompilerParams(dimension_semantics=("parallel",),
                                             vmem_limit_bytes=_VMEM_LIMIT),
        name=f"pre_l{layer}",
    )(x, g_mix, w_in, w_in, w_in, w_in, w_qvt, b_gate, ln_v_g, ln_v_b, w_s, b_s_t, w_pa)


def _attn_kernel(qt_ref, k_ref, vt_ref, lq1_ref, lk1_ref, lq2_ref, lk2_ref, gsub_ref, lam0_ref, o_ref,
                 sa_sc, sb_sc, mt_sc, m_sc, acc_sc):
    n_q, _, tq = qt_ref.shape
    tk = vt_ref.shape[2]
    tc = min(_TQ_CHAIN, tq)
    n_chain = tq // tc
    bufs = (sa_sc, sb_sc)
    lam_init = lam0_ref[...]
    lam = (jnp.exp(jnp.sum(lq1_ref[...] * lk1_ref[...], axis=-1, keepdims=True))
           - jnp.exp(jnp.sum(lq2_ref[...] * lk2_ref[...], axis=-1, keepdims=True)) + lam_init)
    ones = jnp.ones((_ONES_ROWS, tk), _MXU_DTYPE)

    def n_keys(c, diagonal):
        return (c + 1) * tc if diagonal else tk

    def stage(i, j, b):
        for c in range(n_chain):
            nk = n_keys(c, i == j)
            qt = qt_ref[i, :, c * tc:(c + 1) * tc]
            sub = lax.broadcasted_iota(jnp.int32, qt.shape, 0)
            zero = jnp.zeros_like(qt)
            q2 = jnp.concatenate([jnp.where(sub < B_HEAD_DIM, qt, zero), jnp.where(sub >= B_HEAD_DIM, qt, zero)],
                                 axis=1)
            s = _dot(k_ref[j * tk:j * tk + nk, :], q2)
            if i == j:
                kpos = lax.broadcasted_iota(jnp.int32, s.shape, 0)
                qpos = c * tc + jnp.bitwise_and(lax.broadcasted_iota(jnp.int32, s.shape, 1), tc - 1)
                s = jnp.where(kpos <= qpos, s, -jnp.inf)
            bufs[b][c, :nk] = s
            mt_sc[b, c] = jnp.max(s, axis=0, keepdims=True)

    def fold(i, j, b):
        for c in range(n_chain):
            nk = n_keys(c, i == j)
            v_aug = jnp.concatenate([vt_ref[j, :, :nk], ones[:, :nk]], axis=0)
            if i == j:
                m_new = mt_sc[b, c]
                acc_sc[c] = _dot(v_aug, jnp.exp2((bufs[b][c, :nk] - m_new).astype(_MXU_DTYPE)))
            else:
                m = m_sc[c]
                m_new = jnp.maximum(m, mt_sc[b, c])
                p = jnp.exp2((bufs[b][c] - m_new).astype(_MXU_DTYPE))
                acc_sc[c] = jnp.exp2(m - m_new) * acc_sc[c] + _dot(v_aug, p)
            m_sc[c] = m_new

    def finish(i):
        for c in range(n_chain):
            acc = acc_sc[c]
            o = acc[:B_V_DIM] / acc[B_V_DIM:B_V_DIM + 1]
            o = o[:, :tc] - lam * o[:, tc:]
            o = o * lax.rsqrt(jnp.mean(o * o, axis=0, keepdims=True) + EPS) * gsub_ref[...] * (1.0 - lam_init)
            o_ref[i * tq + c * tc:i * tq + (c + 1) * tc, :] = o.T.astype(o_ref.dtype)

    pairs = [(i, j) for i in range(n_q) for j in [i] + list(range(i))]
    stage(*pairs[0], 0)
    for n, (i, j) in enumerate(pairs):
        if n + 1 < len(pairs):
            stage(*pairs[n + 1], (n + 1) % 2)
        fold(i, j, n % 2)
        if n + 1 == len(pairs) or pairs[n + 1][0] != i:
            finish(i)


def _attn_call(qt, k, vt, lam_init, lam_q1, lam_k1, lam_q2, lam_k2, g_sub_col):
    bsz, s_len, _ = k.shape
    tk = vt.shape[2]
    tq = tk
    tc = min(_TQ_CHAIN, tq)
    n_chain = tq // tc
    assert tc & (tc - 1) == 0
    slabs = s_len // tk
    kspec = pl.BlockSpec((None, s_len, B_V_DIM), lambda b, h: (b, 0, h))
    vspec = pl.BlockSpec((slabs, B_V_DIM, tk), lambda b, h: (b, h, 0))
    whole = lambda a: pl.BlockSpec(a.shape, lambda b, h: (0,) * a.ndim)
    return pl.pallas_call(
        _attn_kernel,
        grid=(bsz, B_HEADS),
        in_specs=[vspec, kspec, vspec, whole(lam_q1), whole(lam_k1), whole(lam_q2), whole(lam_k2),
                  whole(g_sub_col), whole(lam_init)],
        out_specs=kspec,
        out_shape=jax.ShapeDtypeStruct((bsz, s_len, B_WIDTH), _MXU_DTYPE),
        scratch_shapes=[pltpu.VMEM((n_chain, tk, 2 * tc), jnp.float32),
                        pltpu.VMEM((n_chain, tk, 2 * tc), jnp.float32),
                        pltpu.VMEM((2, n_chain, 1, 2 * tc), jnp.float32),
                        pltpu.VMEM((n_chain, 1, 2 * tc), jnp.float32),
                        pltpu.VMEM((n_chain, B_V_DIM + _ONES_ROWS, 2 * tc), jnp.float32)],
        compiler_params=pltpu.CompilerParams(dimension_semantics=("parallel", "parallel"),
                                             vmem_limit_bytes=_VMEM_LIMIT),
        name="attn",
    )(qt, k, vt, lam_q1, lam_k1, lam_q2, lam_k2, g_sub_col, lam_init)


def _post_kernel(x_ref, ma_ref, gb_ref, yb_ref, p_ref, wpb_ref, wo_ref, gffn_ref, wup_ref, cw_ref, cb_ref,
                 wdown_ref, gple_ref, wpg_ref, bpg_ref, wple_ref, gfin_ref, o_ref, halo_sc, act_sc,
                 *, tiles_per_seq, final):
    tm = x_ref.shape[0]

    @pl.when(pl.program_id(0) % tiles_per_seq == 0)
    def _():
        halo_sc[...] = jnp.zeros_like(halo_sc)

    pb = _dot(yb_ref[...], wpb_ref[...])
    merged = ma_ref[...].astype(jnp.float32) + gb_ref[...].astype(jnp.float32) * pb
    x1 = x_ref[...] + _dot(merged.astype(_MXU_DTYPE), wo_ref[...])
    e = _dot(p_ref[...].astype(_MXU_DTYPE), wple_ref[...])

    h2 = _rms(x1, gffn_ref[...]).astype(_MXU_DTYPE)
    top = lax.broadcasted_iota(jnp.int32, (_HALO, _FF_CHUNK), 0)

    def conv(lo):
        a = _dot(h2, wup_ref[:, lo:lo + _FF_CHUNK])
        w0, w1, w2 = (cw_ref[j:j + 1, lo:lo + _FF_CHUNK] for j in range(CONV_W))
        b = cb_ref[:, lo:lo + _FF_CHUNK]
        y = b + pltpu.roll(a, 2, axis=0) * w0 + pltpu.roll(a, 1, axis=0) * w1 + a * w2
        prev, head = halo_sc[:, lo:lo + _FF_CHUNK], a[:_HALO]
        r2 = jnp.where(top < 2, pltpu.roll(prev, 2, axis=0), pltpu.roll(head, 2, axis=0))
        r1 = jnp.where(top < 1, pltpu.roll(prev, 1, axis=0), pltpu.roll(head, 1, axis=0))
        y_head = b + r2 * w0 + r1 * w1 + head * w2
        halo_sc[:, lo:lo + _FF_CHUNK] = a[tm - _HALO:]
        return y, y_head

    for j in range(D_FF // _FF_CHUNK):
        lo = j * _FF_CHUNK
        yg, yg_head = conv(lo)
        yv, yv_head = conv(D_FF + lo)
        act_sc[:, lo:lo + _FF_CHUNK] = (_gelu(yg) * yv).astype(_MXU_DTYPE)
        act_sc[:_HALO, lo:lo + _FF_CHUNK] = (_gelu(yg_head) * yv_head).astype(_MXU_DTYPE)
    x2 = x1 + _dot(act_sc[...], wdown_ref[...])

    pg = jax.nn.sigmoid(_dot(_rms(x2, gple_ref[...]).astype(_MXU_DTYPE), wpg_ref[...]) + bpg_ref[...])
    x3 = x2 + pg * e
    if final:
        x3 = _rms(x3, gfin_ref[...])
    o_ref[...] = x3


def _post_call(x, ma, gb, yb, p, layer, s_len, final, w_pb, w_o, g_ffn, w_up, conv_w, conv_b, w_down, g_ple,
               w_pg, b_pg, w_ple, g_final):
    t = x.shape[0]
    tm = min(_TM_POST, s_len)
    tok = lambda w: pl.BlockSpec((tm, w), lambda i: (i, 0))
    return pl.pallas_call(
        functools.partial(_post_kernel, tiles_per_seq=s_len // tm, final=final),
        grid=(t // tm,),
        in_specs=[tok(D_MODEL), tok(D_MODEL), tok(D_MODEL), tok(B_WIDTH),
                  pl.BlockSpec((None, tm, PLE_DIM), lambda i: (layer, i, 0)),
                  _resident((B_WIDTH, D_MODEL), layer),
                  _resident((D_MODEL, D_MODEL), layer),
                  _resident((1, D_MODEL), layer),
                  _resident((D_MODEL, 2 * D_FF), layer),
                  _resident((CONV_W, 2 * D_FF), layer),
                  _resident((1, 2 * D_FF), layer),
                  _resident((D_FF, D_MODEL), layer),
                  _resident((1, D_MODEL), layer),
                  _resident((D_MODEL, D_MODEL), layer),
                  _resident((1, D_MODEL), layer),
                  _resident((PLE_DIM, D_MODEL), layer),
                  pl.BlockSpec((1, D_MODEL), lambda i: (0, 0))],
        out_specs=tok(D_MODEL),
        out_shape=jax.ShapeDtypeStruct((t, D_MODEL), jnp.float32),
        scratch_shapes=[pltpu.VMEM((_HALO, 2 * D_FF), jnp.float32),
                        pltpu.VMEM((tm, D_FF), _MXU_DTYPE)],
        compiler_params=pltpu.CompilerParams(dimension_semantics=("arbitrary",),
                                             vmem_limit_bytes=_VMEM_LIMIT),
        name=f"post_l{layer}",
    )(x, ma, gb, yb, p, w_pb, w_o, g_ffn, w_up, conv_w, conv_b, w_down, g_ple, w_pg, b_pg, w_ple, g_final)


def kernel(x, p, g_mix, w_in, b_gate, ln_v_g, ln_v_b, w_s, b_s, lam_q1, lam_k1, lam_q2, lam_k2, g_sub, w_pa,
           w_pb, w_o, g_ffn, w_up, conv_w, conv_b, w_down, g_ple, w_pg, b_pg, w_ple, g_final):
    bsz, s_len, _ = x.shape
    depth = w_in.shape[0]
    t = bsz * s_len
    mx = lambda w: w.astype(_MXU_DTYPE)
    row = lambda a: a[:, None, :]
    w_in = mx(w_in)
    w_qvt = jnp.swapaxes(jnp.concatenate([w_in[:, :, O_Q:O_K], w_in[:, :, O_V:O_G]], axis=2), 1, 2)
    w_pa, w_pb, w_o, w_up, w_down, w_pg, w_ple = map(mx, (w_pa, w_pb, w_o, w_up, w_down, w_pg, w_ple))
    g_mix, b_gate, ln_v_g, ln_v_b, g_ffn, conv_b, g_ple, b_pg = map(
        row, (g_mix, b_gate, ln_v_g, ln_v_b, g_ffn, conv_b, g_ple, b_pg))
    lam_q1, lam_k1, lam_q2, lam_k2 = map(row, (lam_q1, lam_k1, lam_q2, lam_k2))
    g_sub_col = g_sub[:, :, None]
    b_s_t = jnp.swapaxes(b_s, 1, 2)
    g_final = g_final[None, :]
    p = p.reshape(depth, t, PLE_DIM)
    x = x.reshape(t, D_MODEL)
    tm_pre = min(_TM_PRE, s_len)
    for i in range(depth):
        lam_init = 0.8 - 0.6 * math.exp(-0.3 * i)
        qt, k, vt, ma, gb = _pre_call(x, i, tm_pre, g_mix, w_in, w_qvt, b_gate, ln_v_g, ln_v_b, w_s, b_s_t, w_pa)
        yb = _attn_call(qt, k.reshape(bsz, s_len, B_QK_WIDTH), vt, jnp.full((1, 1), lam_init, jnp.float32),
                        lam_q1[i], lam_k1[i], lam_q2[i], lam_k2[i], g_sub_col[i])
        x = _post_call(x, ma, gb, yb.reshape(t, B_WIDTH), p, i, s_len, i == depth - 1, w_pb, w_o, g_ffn, w_up,
                       conv_w, conv_b, w_down, g_ple, w_pg, b_pg, w_ple, g_final)
    return x.reshape(bsz, s_len, D_MODEL)
```
